```python
import math
import jax
import jax.numpy as jnp
from jax import lax
import numpy as np

D_MODEL = 1024
BATCH = 4
SEQ = 4096
DEPTH = 4
DEC_BATCH = 128
DEC_SEQ = 8
PAST_LEN = 2048
PAGE_SIZE = 128

N_MIXERS = 2
N_SSM_LAYERS = (DEPTH + 1) // 2
N_ATT_LAYERS = DEPTH // 2
SSM_WIDTH = D_MODEL
SSM_GROUP = 16
SSM_GROUPS = SSM_WIDTH // SSM_GROUP
SSM_STATE = 64
DT_MIN = 0.001
DT_MAX = 0.1
ATT_HEADS = 16
HEAD_DIM = 64
ATT_WIDTH = ATT_HEADS * HEAD_DIM
WINDOWS = (128, 512, 2048)
DILATIONS = (1, 4, 16)
N_ATT_GROUPS = len(WINDOWS)
ATT_QKV = 3 * N_ATT_GROUPS * ATT_WIDTH
ATT_IN = ATT_QKV + ATT_WIDTH
ALIBI_BASE_EXP = 8.0
DEEPNORM_ALPHA = (2 * DEPTH) ** 0.25
DEEPNORM_BETA = (8 * DEPTH) ** -0.25
LN_EPS = 1e-5

kernel_name = 'hybrid_s5_dilated_attn_decoder_step'


def _layernorm(x, g, b):
    xf = x.astype(jnp.float32)
    mu = xf.mean(-1, keepdims=True)
    var = jnp.square(xf - mu).mean(-1, keepdims=True)
    y = (xf - mu) * lax.rsqrt(var + LN_EPS) * g.astype(jnp.float32) + b.astype(jnp.float32)
    return y.astype(x.dtype)


def _adaln(c, w, b):
    mod = jnp.einsum('bd,df->bf', jax.nn.silu(c), w) + b
    shift, scale, gate = jnp.split(mod[:, None, :], 3, axis=-1)
    return shift, scale, gate


def _post_norm(x, out, gate, g, b):
    return _layernorm(DEEPNORM_ALPHA * x + gate * out, g, b)


def _zoh(a_re, a_im, log_dt, b_re, b_im):
    f32 = jnp.float32
    dt = jnp.exp(log_dt.astype(f32))[:, None]
    lr = a_re.astype(f32)
    li = a_im.astype(f32)
    mag = jnp.exp(lr * dt)
    abar_re = mag * jnp.cos(li * dt)
    abar_im = mag * jnp.sin(li * dt)
    den = lr * lr + li * li
    xr = abar_re - 1.0
    g_re = (xr * lr + abar_im * li) / den
    g_im = (abar_im * lr - xr * li) / den
    br = b_re.astype(f32)
    bi = b_im.astype(f32)
    bbar_re = g_re[..., None] * br - g_im[..., None] * bi
    bbar_im = g_re[..., None] * bi + g_im[..., None] * br
    return abar_re, abar_im, bbar_re, bbar_im


def _complex_affine_combine(e1, e2):
    a1r, a1i, b1r, b1i = e1
    a2r, a2i, b2r, b2i = e2
    return (a1r * a2r - a1i * a2i,
            a1r * a2i + a1i * a2r,
            a2r * b1r - a2i * b1i + b2r,
            a2r * b1i + a2i * b1r + b2i)


def _ssm_branch(u, w_in, a_re, a_im, log_dt, b_re, b_im, c_re, c_im, d_skip, w_glu, b_glu, w_out, h0=None):
    bn, L, _ = u.shape
    proj = jnp.einsum('bld,df->blf', u, w_in)
    s_in, z = jnp.split(proj, 2, axis=-1)
    abar_re, abar_im, bbar_re, bbar_im = _zoh(a_re, a_im, log_dt, b_re, b_im)
    xs = s_in.astype(jnp.float32).reshape(bn, L, SSM_GROUPS, SSM_GROUP)
    bu_re = jnp.einsum('blgc,gpc->blgp', xs, bbar_re)
    bu_im = jnp.einsum('blgc,gpc->blgp', xs, bbar_im)
    if h0 is not None:
        h0_re = h0[0].astype(jnp.float32)
        h0_im = h0[1].astype(jnp.float32)
        bu_re = bu_re.at[:, 0].add(abar_re * h0_re - abar_im * h0_im)
        bu_im = bu_im.at[:, 0].add(abar_re * h0_im + abar_im * h0_re)
    shape = (1, L, SSM_GROUPS, SSM_STATE)
    elems = (jnp.broadcast_to(abar_re, shape), jnp.broadcast_to(abar_im, shape), bu_re, bu_im)
    _, _, h_re, h_im = lax.associative_scan(_complex_affine_combine, elems, axis=1)
    y = (jnp.einsum('blgp,gcp->blgc', h_re, c_re.astype(jnp.float32))
         - jnp.einsum('blgp,gcp->blgc', h_im, c_im.astype(jnp.float32)))
    y = y.reshape(bn, L, SSM_WIDTH) + d_skip.astype(jnp.float32) * xs.reshape(bn, L, SSM_WIDTH)
    y = jax.nn.gelu(y)
    y = y * jax.nn.sigmoid(jnp.einsum('ble,ef->blf', y, w_glu.astype(jnp.float32)) + b_glu.astype(jnp.float32))
    y = y.astype(u.dtype) * jax.nn.silu(z)
    out = jnp.einsum('ble,ed->bld', y, w_out)
    return out, h_re[:, -1], h_im[:, -1]


def _alibi_slopes():
    h = jnp.arange(ATT_HEADS, dtype=jnp.float32) + 1.0
    return jnp.exp2(-ALIBI_BASE_EXP * h / ATT_HEADS)


def _split_attn_proj(u, w_in):
    bn, L, _ = u.shape
    proj = jnp.einsum('bld,df->blf', u, w_in)
    qkv = proj[..., :ATT_QKV].reshape(bn, L, N_ATT_GROUPS, 3, ATT_HEADS, HEAD_DIM)
    z = proj[..., ATT_QKV:]
    return qkv, z


def _dilated_prompt(q, k, v, window, dil, slopes):
    bn, S, H, E = q.shape
    blk = window // dil
    s_pad = -(-S // window) * window
    nb = s_pad // window
    U = s_pad // dil
    pad = ((0, 0), (0, s_pad - S), (0, 0), (0, 0))

    def to_res(t):
        t = jnp.pad(t, pad).reshape(bn, U, dil, H, E).transpose(0, 2, 1, 3, 4)
        return t.reshape(bn, dil, nb, blk, H, E)

    def with_prev(t):
        prev = jnp.concatenate([jnp.zeros_like(t[:, :, :1]), t[:, :, :-1]], axis=2)
        return jnp.concatenate([prev, t], axis=3)

    qr = to_res(q)
    kb = with_prev(to_res(k))
    vb = with_prev(to_res(v))
    s = jnp.einsum('brnqhe,brnkhe->brnhqk', qr, kb, preferred_element_type=jnp.float32) * (HEAD_DIM ** -0.5)
    a = jnp.arange(blk)[:, None]
    bk = jnp.arange(2 * blk)[None, :]
    steps = a - bk + blk
    key_ok = (jnp.arange(nb)[:, None, None] > 0) | (bk[None] >= blk)
    ok = (steps >= 0) & (steps <= blk) & key_ok
    bias = -slopes[:, None, None] * (steps * dil).astype(jnp.float32)[None]
    s = jnp.where(ok[:, None], s + bias, -jnp.inf)
    m = s.max(-1, keepdims=True)
    p = jnp.exp(s - m)
    l = p.sum(-1, keepdims=True)
    o = jnp.einsum('brnhqk,brnkhe->brnqhe', p, vb)
    o = o / jnp.swapaxes(l[..., 0], 3, 4)[..., None]
    lse = jnp.swapaxes((m + jnp.log(l))[..., 0], 3, 4)
    o = o.reshape(bn, dil, U, H, E).transpose(0, 2, 1, 3, 4).reshape(bn, s_pad, H, E)[:, :S]
    lse = lse.reshape(bn, dil, U, H).transpose(0, 2, 1, 3).reshape(bn, s_pad, H)[:, :S]
    return o, lse


def _dilated_sample(q, k_new, v_new, kv_buf, window, dil, slopes):
    w_eff = kv_buf.shape[1]
    T = q.shape[1]
    k_all = jnp.concatenate([kv_buf[:, :, 0], k_new], axis=1)
    v_all = jnp.concatenate([kv_buf[:, :, 1], v_new], axis=1)
    steps = jnp.arange(window // dil + 1)
    rows = w_eff + jnp.arange(T)[:, None] - steps[None, :] * dil
    ok = rows >= 0
    rows_c = jnp.maximum(rows, 0)
    kg = jnp.take(k_all, rows_c, axis=1)
    s = jnp.einsum('bthe,btkhe->bhtk', q, kg, preferred_element_type=jnp.float32) * (HEAD_DIM ** -0.5)
    s = s - slopes[:, None, None] * (steps * dil).astype(jnp.float32)[None, None, :]
    s = jnp.where(ok, s, -jnp.inf)
    m = s.max(-1, keepdims=True)
    p = jnp.exp(s - m)
    l = p.sum(-1, keepdims=True)
    vg = jnp.take(v_all, rows_c, axis=1)
    o = jnp.einsum('bhtk,btkhe->bthe', p, vg) / jnp.swapaxes(l[..., 0], 1, 2)[..., None]
    lse = jnp.swapaxes((m + jnp.log(l))[..., 0], 1, 2)
    return o, lse


def _merge_groups(outs, lses):
    w = jax.nn.softmax(jnp.stack(lses, 0), axis=0)
    return jnp.einsum('gblh,gblhe->blhe', w, jnp.stack(outs, 0))


def _attn_out(u, o, z, w_out):
    bn, L = u.shape[0], u.shape[1]
    y = o.reshape(bn, L, ATT_WIDTH).astype(u.dtype) * jax.nn.silu(z)
    return jnp.einsum('ble,ed->bld', y, w_out)


def _attn_branch_prompt(u, w_in, w_out):
    qkv, z = _split_attn_proj(u, w_in)
    slopes = _alibi_slopes()
    S = u.shape[1]
    outs, lses, rows = [], [], []
    for g in range(N_ATT_GROUPS):
        q, k, v = qkv[:, :, g, 0], qkv[:, :, g, 1], qkv[:, :, g, 2]
        o, lse = _dilated_prompt(q, k, v, WINDOWS[g], DILATIONS[g], slopes)
        outs.append(o)
        lses.append(lse)
        keep = min(WINDOWS[g], S)
        rows.append(jnp.stack([k[:, S - keep:], v[:, S - keep:]], axis=2))
    return _attn_out(u, _merge_groups(outs, lses), z, w_out), rows


def _attn_branch_sample(u, kv_bufs, w_in, w_out):
    qkv, z = _split_attn_proj(u, w_in)
    slopes = _alibi_slopes()
    outs, lses, rows = [], [], []
    for g in range(N_ATT_GROUPS):
        q, k, v = qkv[:, :, g, 0], qkv[:, :, g, 1], qkv[:, :, g, 2]
        o, lse = _dilated_sample(q, k, v, kv_bufs[g], WINDOWS[g], DILATIONS[g], slopes)
        outs.append(o)
        lses.append(lse)
        rows.append(jnp.stack([k, v], axis=2))
    return _attn_out(u, _merge_groups(outs, lses), z, w_out), rows


def setup_inputs(seed: int = 0) -> dict:
    key = jax.random.key(seed)
    keys = jax.random.split(key, 32)
    nxt = iter(range(32))

    def nrm(shape, scale):
        return scale * jax.random.normal(keys[next(nxt)], shape, jnp.float32)

    D = D_MODEL
    x_prompt = nrm((BATCH, SEQ, D), 1.0)
    x_sample = nrm((DEC_BATCH, DEC_SEQ, D), 1.0)
    c_prompt = nrm((BATCH, D), 1.0)
    c_sample = nrm((DEC_BATCH, D), 1.0)
    state_ssm_re = nrm((N_SSM_LAYERS, DEC_BATCH, SSM_GROUPS, SSM_STATE), 0.1)
    state_ssm_im = nrm((N_SSM_LAYERS, DEC_BATCH, SSM_GROUPS, SSM_STATE), 0.1)
    cache_kv_w128 = nrm((N_ATT_LAYERS, DEC_BATCH, min(WINDOWS[0], PAST_LEN), 2, ATT_HEADS, HEAD_DIM), 1.0)
    cache_kv_w512 = nrm((N_ATT_LAYERS, DEC_BATCH, min(WINDOWS[1], PAST_LEN), 2, ATT_HEADS, HEAD_DIM), 1.0)
    cache_kv_w2048 = nrm((N_ATT_LAYERS, DEC_BATCH, min(WINDOWS[2], PAST_LEN), 2, ATT_HEADS, HEAD_DIM), 1.0)
    w_ada = nrm((DEPTH, D, 3 * D), 0.1 * D ** -0.5)
    b_ada = jnp.concatenate([nrm((DEPTH, 2 * D), 0.02), 1.0 + nrm((DEPTH, D), 0.02)], axis=-1)
    ln_g = 1.0 + nrm((DEPTH, D), 0.02)
    ln_b = nrm((DEPTH, D), 0.02)
    ssm_w_in = nrm((N_SSM_LAYERS, D, 2 * SSM_WIDTH), D ** -0.5)
    ssm_a_re = -0.5 * jnp.exp(nrm((N_SSM_LAYERS, SSM_GROUPS, SSM_STATE), 0.01))
    ssm_a_im = math.pi * jnp.arange(SSM_STATE, dtype=jnp.float32) + nrm((N_SSM_LAYERS, SSM_GROUPS, SSM_STATE), 0.01)
    ssm_log_dt = jax.random.uniform(keys[next(nxt)], (N_SSM_LAYERS, SSM_GROUPS), jnp.float32,
                                    math.log(DT_MIN), math.log(DT_MAX))
    ssm_b_re = nrm((N_SSM_LAYERS, SSM_GROUPS, SSM_STATE, SSM_GROUP), (2 * SSM_GROUP) ** -0.5)
    ssm_b_im = nrm((N_SSM_LAYERS, SSM_GROUPS, SSM_STATE, SSM_GROUP), (2 * SSM_GROUP) ** -0.5)
    ssm_c_re = nrm((N_SSM_LAYERS, SSM_GROUPS, SSM_GROUP, SSM_STATE), 0.5 ** 0.5)
    ssm_c_im = nrm((N_SSM_LAYERS, SSM_GROUPS, SSM_GROUP, SSM_STATE), 0.5 ** 0.5)
    ssm_d = nrm((N_SSM_LAYERS, SSM_WIDTH), 1.0)
    ssm_w_glu = nrm((N_SSM_LAYERS, SSM_WIDTH, SSM_WIDTH), SSM_WIDTH ** -0.5)
    ssm_b_glu = nrm((N_SSM_LAYERS, SSM_WIDTH), 0.02)
    ssm_w_out = nrm((N_SSM_LAYERS, SSM_WIDTH, D), DEEPNORM_BETA * SSM_WIDTH ** -0.5)
    attn_w_in = nrm((N_ATT_LAYERS, D, ATT_IN), D ** -0.5)
    attn_w_out = nrm((N_ATT_LAYERS, ATT_WIDTH, D), DEEPNORM_BETA * ATT_WIDTH ** -0.5)
    return {'x_prompt': x_prompt, 'x_sample': x_sample, 'c_prompt': c_prompt, 'c_sample': c_sample,
            'state_ssm_re': state_ssm_re, 'state_ssm_im': state_ssm_im,
            'cache_kv_w128': cache_kv_w128, 'cache_kv_w512': cache_kv_w512, 'cache_kv_w2048': cache_kv_w2048,
            'w_ada': w_ada, 'b_ada': b_ada, 'ln_g': ln_g, 'ln_b': ln_b,
            'ssm_w_in': ssm_w_in, 'ssm_a_re': ssm_a_re, 'ssm_a_im': ssm_a_im, 'ssm_log_dt': ssm_log_dt,
            'ssm_b_re': ssm_b_re, 'ssm_b_im': ssm_b_im, 'ssm_c_re': ssm_c_re, 'ssm_c_im': ssm_c_im,
            'ssm_d': ssm_d, 'ssm_w_glu': ssm_w_glu, 'ssm_b_glu': ssm_b_glu, 'ssm_w_out': ssm_w_out,
            'attn_w_in': attn_w_in, 'attn_w_out': attn_w_out}


def reference(x_prompt, x_sample, c_prompt, c_sample, state_ssm_re, state_ssm_im,
              cache_kv_w128, cache_kv_w512, cache_kv_w2048, w_ada, b_ada, ln_g, ln_b,
              ssm_w_in, ssm_a_re, ssm_a_im, ssm_log_dt, ssm_b_re, ssm_b_im, ssm_c_re, ssm_c_im,
              ssm_d, ssm_w_glu, ssm_b_glu, ssm_w_out, attn_w_in, attn_w_out):
    xp, xs = x_prompt, x_sample
    ssm_p_re, ssm_p_im, ssm_s_re, ssm_s_im = [], [], [], []
    kv_p = [[] for _ in range(N_ATT_GROUPS)]
    kv_s = [[] for _ in range(N_ATT_GROUPS)]
    for i in range(DEPTH):
        j = i // N_MIXERS
        sh_p, sc_p, gt_p = _adaln(c_prompt, w_ada[i], b_ada[i])
        sh_s, sc_s, gt_s = _adaln(c_sample, w_ada[i], b_ada[i])
        up = xp * (1.0 + sc_p) + sh_p
        us = xs * (1.0 + sc_s) + sh_s
        if i % N_MIXERS == 0:
            ssm_w = (ssm_w_in[j], ssm_a_re[j], ssm_a_im[j], ssm_log_dt[j], ssm_b_re[j], ssm_b_im[j],
                     ssm_c_re[j], ssm_c_im[j], ssm_d[j], ssm_w_glu[j], ssm_b_glu[j], ssm_w_out[j])
            op, hpr, hpi = _ssm_branch(up, *ssm_w)
            os_, hsr, hsi = _ssm_branch(us, *ssm_w, h0=(state_ssm_re[j], state_ssm_im[j]))
            ssm_p_re.append(hpr)
            ssm_p_im.append(hpi)
            ssm_s_re.append(hsr)
            ssm_s_im.append(hsi)
        else:
            op, rows_p = _attn_branch_prompt(up, attn_w_in[j], attn_w_out[j])
            os_, rows_s = _attn_branch_sample(us, (cache_kv_w128[j], cache_kv_w512[j], cache_kv_w2048[j]),
                                              attn_w_in[j], attn_w_out[j])
            for g in range(N_ATT_GROUPS):
                kv_p[g].append(rows_p[g])
                kv_s[g].append(rows_s[g])
        xp = _post_norm(xp, op, gt_p, ln_g[i], ln_b[i])
        xs = _post_norm(xs, os_, gt_s, ln_g[i], ln_b[i])
    y_prompt, y_sample = xp, xs
    ssm_re_prompt = jnp.stack(ssm_p_re, 0)
    ssm_im_prompt = jnp.stack(ssm_p_im, 0)
    kv_w128_prompt = jnp.stack(kv_p[0], 0)
    kv_w512_prompt = jnp.stack(kv_p[1], 0)
    kv_w2048_prompt = jnp.stack(kv_p[2], 0)
    ssm_re_sample = jnp.stack(ssm_s_re, 0)
    ssm_im_sample = jnp.stack(ssm_s_im, 0)
    kv_w128_sample = jnp.stack(kv_s[0], 0)
    kv_w512_sample = jnp.stack(kv_s[1], 0)
    kv_w2048_sample = jnp.stack(kv_s[2], 0)
    return (y_prompt, y_sample, ssm_re_prompt, ssm_im_prompt, kv_w128_prompt, kv_w512_prompt, kv_w2048_prompt,
            ssm_re_sample, ssm_im_sample, kv_w128_sample, kv_w512_sample, kv_w2048_sample)
```

```python
import functools
import math

import jax
import jax.numpy as jnp
from jax import lax
from jax.experimental import pallas as pl
from jax.experimental.pallas import tpu as pltpu

F32 = jnp.float32
BF16 = jnp.bfloat16

D_MODEL = 1024
DEPTH = 4
SSM_GROUPS = 64
SSM_GROUP = 16
SSM_STATE = 64
ATT_HEADS = 16
HEAD_DIM = 64
WINDOWS = (128, 512, 2048)
DILATIONS = (1, 4, 16)
ATT_QKV = 3 * 3 * ATT_HEADS * HEAD_DIM
ALIBI_BASE_EXP = 8.0
DEEPNORM_ALPHA = (2 * DEPTH) ** 0.25
LN_EPS = 1e-5
QK_SCALE = HEAD_DIM ** -0.5

LANE_TILE = 128
BLK = 128
SSM_CHUNK = 128
SSM_CENTER = SSM_CHUNK // 2
SSM_KT = D_MODEL // LANE_TILE
GROUPS_PER_KT = LANE_TILE // SSM_GROUP
KT_STATE = GROUPS_PER_KT * SSM_STATE
VMEM_LIMIT = 56 * 1024 * 1024


def _cparams(sem):
    return pltpu.CompilerParams(dimension_semantics=sem, vmem_limit_bytes=VMEM_LIMIT)


def _ada_body(c_ref, w_ref, b_ref, o_ref):
    c = c_ref[...]
    s = c * jax.nn.sigmoid(c)
    o_ref[0] = jnp.dot(s.astype(BF16), w_ref[0].astype(BF16), preferred_element_type=F32) + b_ref[0]


def _ada(c_all, w_ada, b_ada):
    rows = c_all.shape[0]
    d = D_MODEL
    return pl.pallas_call(
        _ada_body,
        grid=(DEPTH, 3),
        in_specs=[pl.BlockSpec((rows, d), lambda i, n: (0, 0)),
                  pl.BlockSpec((1, d, d), lambda i, n: (i, 0, n)),
                  pl.BlockSpec((1, 1, d), lambda i, n: (i, 0, n))],
        out_specs=pl.BlockSpec((1, rows, d), lambda i, n: (i, 0, n)),
        out_shape=jax.ShapeDtypeStruct((DEPTH, rows, 3 * d), F32),
        compiler_params=_cparams(("arbitrary", "arbitrary")),
        name="ada",
    )(c_all, w_ada, b_ada.reshape(DEPTH, 1, 3 * d))


def _modmm_body(x_ref, sc_ref, sh_ref, w_ref, o_ref, u_scr):
    @pl.when(pl.program_id(2) == 0)
    def _():
        u_scr[...] = (x_ref[0] * (1.0 + sc_ref[0]) + sh_ref[0]).astype(BF16)

    o_ref[0] = jnp.dot(u_scr[...], w_ref[...], preferred_element_type=F32)


def _mod_index(per_token):
    if per_token:
        return lambda b, m, n: (b, m, 0)
    return lambda b, m, n: (b, 0, 0)


def _modmm(x, sc, sh, w_bf, tm, tn):
    bsz, length, d = x.shape
    n_out = w_bf.shape[1]
    per_token = sc.shape[1] != 1
    mod_rows = tm if per_token else 1
    return pl.pallas_call(
        _modmm_body,
        grid=(bsz, length // tm, n_out // tn),
        in_specs=[pl.BlockSpec((1, tm, d), lambda b, m, n: (b, m, 0)),
                  pl.BlockSpec((1, mod_rows, d), _mod_index(per_token)),
                  pl.BlockSpec((1, mod_rows, d), _mod_index(per_token)),
                  pl.BlockSpec((d, tn), lambda b, m, n: (0, n))],
        out_specs=pl.BlockSpec((1, tm, tn), lambda b, m, n: (b, m, n)),
        out_shape=jax.ShapeDtypeStruct((bsz, length, n_out), F32),
        scratch_shapes=[pltpu.VMEM((tm, d), BF16)],
        compiler_params=_cparams(("arbitrary", "arbitrary", "arbitrary")),
        name="mod_proj",
    )(x, sc, sh, w_bf)


def _tail_body(*refs, glu):
    if glu:
        (x_ref, sc_ref, sh_ref, gt_ref, y_ref, wz_ref, wglu_ref, bglu_ref, wout_ref,
         lng_ref, lnb_ref, o_ref) = refs
    else:
        (x_ref, sc_ref, sh_ref, gt_ref, y_ref, wz_ref, wout_ref, lng_ref, lnb_ref, o_ref) = refs
    x = x_ref[0]
    u = (x * (1.0 + sc_ref[0]) + sh_ref[0]).astype(BF16)
    z = jnp.dot(u, wz_ref[...], preferred_element_type=F32)
    y = y_ref[0]
    if glu:
        y = y * jax.nn.sigmoid(
            jnp.dot(y.astype(BF16), wglu_ref[...], preferred_element_type=F32) + bglu_ref[...])
    y = y * (z * jax.nn.sigmoid(z))
    out = jnp.dot(y.astype(BF16), wout_ref[...], preferred_element_type=F32)
    r = DEEPNORM_ALPHA * x + gt_ref[0] * out
    mu = jnp.mean(r, axis=-1, keepdims=True)
    rc = r - mu
    var = jnp.mean(rc * rc, axis=-1, keepdims=True)
    o_ref[0] = rc * lax.rsqrt(var + LN_EPS) * lng_ref[...] + lnb_ref[...]


def _tail(x, sc, sh, gt, y, wz_bf, wout_bf, ln_g, ln_b, tm, wglu_bf=None, bglu=None):
    bsz, length, d = x.shape
    per_token = sc.shape[1] != 1
    mod_rows = tm if per_token else 1
    if per_token:
        mod_idx = lambda b, m: (b, m, 0)
    else:
        mod_idx = lambda b, m: (b, 0, 0)
    row_spec = pl.BlockSpec((1, tm, d), lambda b, m: (b, m, 0))
    mod_spec = pl.BlockSpec((1, mod_rows, d), mod_idx)
    w_spec = pl.BlockSpec((d, d), lambda b, m: (0, 0))
    vec_spec = pl.BlockSpec((1, d), lambda b, m: (0, 0))
    glu = wglu_bf is not None
    operands = [x, sc, sh, gt, y, wz_bf]
    in_specs = [row_spec, mod_spec, mod_spec, mod_spec, row_spec, w_spec]
    if glu:
        operands += [wglu_bf, bglu.reshape(1, d)]
        in_specs += [w_spec, vec_spec]
    operands += [wout_bf, ln_g.reshape(1, d), ln_b.reshape(1, d)]
    in_specs += [w_spec, vec_spec, vec_spec]
    return pl.pallas_call(
        functools.partial(_tail_body, glu=glu),
        grid=(bsz, length // tm),
        in_specs=in_specs,
        out_specs=row_spec,
        out_shape=jax.ShapeDtypeStruct((bsz, length, d), F32),
        compiler_params=_cparams(("arbitrary", "arbitrary")),
        name="tail_glu" if glu else "tail",
    )(*operands)


def _complex_mul(ar, ai, br, bi):
    return ar * br - ai * bi, ar * bi + ai * br


def _prefix_matmul(l_mat, g):
    hi = g.astype(BF16)
    lo = (g - hi.astype(F32)).astype(BF16)
    return (jnp.dot(l_mat, hi, preferred_element_type=F32)
            + jnp.dot(l_mat, lo, preferred_element_type=F32))


def _ssm_scaled_inputs(s_c, bdb, tn):
    ks = KT_STATE
    bu = jnp.dot(s_c.astype(BF16), bdb, preferred_element_type=F32)
    return _complex_mul(tn[:, :ks], tn[:, ks:], bu[:, :ks], bu[:, ks:])


def _ssm_outputs(s_c, p_re, p_im, tp, bdc, dsk):
    ks = KT_STATE
    h_re, h_im = _complex_mul(tp[:, :ks], tp[:, ks:], p_re, p_im)
    y = (jnp.dot(h_re.astype(BF16), bdc[:ks], preferred_element_type=F32)
         + jnp.dot(h_im.astype(BF16), bdc[ks:], preferred_element_type=F32))
    return jax.nn.gelu(y + dsk * s_c), h_re, h_im


def _ssm_prompt_body(s_ref, bdb_ref, bdc_ref, tn_ref, tp_ref, l_ref, cm_ref, d_ref,
                     y_ref, h_ref, carry_scr, *, n_chunks):
    ks = KT_STATE
    t = SSM_CHUNK

    @pl.when(pl.program_id(2) == 0)
    def _():
        carry_scr[...] = jnp.zeros_like(carry_scr)

    l_mat = l_ref[...]
    tn = tn_ref[0]
    tp = tp_ref[0]
    cm = cm_ref[0]
    bdb = bdb_ref[0]
    bdc = bdc_ref[0]
    dsk = d_ref[...]
    row0 = lax.broadcasted_iota(jnp.int32, (t, 1), 0) == 0
    hp = carry_scr[0:1, :]
    for ci in range(n_chunks):
        s_c = s_ref[0, ci * t:(ci + 1) * t, :]
        g_re, g_im = _ssm_scaled_inputs(s_c, bdb, tn)
        c_re, c_im = _complex_mul(cm[:, :ks], cm[:, ks:], hp[:, :ks], hp[:, ks:])
        g_re = jnp.where(row0, g_re + c_re, g_re)
        g_im = jnp.where(row0, g_im + c_im, g_im)
        p_re = _prefix_matmul(l_mat, g_re)
        p_im = _prefix_matmul(l_mat, g_im)
        y, h_re, h_im = _ssm_outputs(s_c, p_re, p_im, tp, bdc, dsk)
        y_ref[0, ci * t:(ci + 1) * t, :] = y
        hp = jnp.concatenate([h_re[t - 1:t], h_im[t - 1:t]], axis=1)
    carry_scr[0:1, :] = hp
    h_ref[0, 0] = jnp.broadcast_to(hp, (8, 2 * ks))


def _ssm_core_prompt(s, prep, tt):
    bsz, length, d = s.shape
    t = SSM_CHUNK
    w2 = 2 * KT_STATE
    y, h = pl.pallas_call(
        functools.partial(_ssm_prompt_body, n_chunks=tt // t),
        grid=(bsz, SSM_KT, length // tt),
        in_specs=[pl.BlockSpec((1, tt, LANE_TILE), lambda b, k, c: (b, c, k)),
                  pl.BlockSpec((1, LANE_TILE, w2), lambda b, k, c: (k, 0, 0)),
                  pl.BlockSpec((1, w2, LANE_TILE), lambda b, k, c: (k, 0, 0)),
                  pl.BlockSpec((1, t, w2), lambda b, k, c: (k, 0, 0)),
                  pl.BlockSpec((1, t, w2), lambda b, k, c: (k, 0, 0)),
                  pl.BlockSpec((t, t), lambda b, k, c: (0, 0)),
                  pl.BlockSpec((1, 1, w2), lambda b, k, c: (k, 0, 0)),
                  pl.BlockSpec((1, LANE_TILE), lambda b, k, c: (0, k))],
        out_specs=[pl.BlockSpec((1, tt, LANE_TILE), lambda b, k, c: (b, c, k)),
                   pl.BlockSpec((1, 1, 8, w2), lambda b, k, c: (b, k, 0, 0))],
        out_shape=[jax.ShapeDtypeStruct((bsz, length, d), F32),
                   jax.ShapeDtypeStruct((bsz, SSM_KT, 8, w2), F32)],
        scratch_shapes=[pltpu.VMEM((8, w2), F32)],
        compiler_params=_cparams(("arbitrary", "arbitrary", "arbitrary")),
        name="ssm_core_prompt",
    )(s, prep["bdb"], prep["bdc"], prep["tn_p"], prep["tp_p"], prep["l_p"], prep["cm_p"],
      prep["dsk"])
    return y, h[:, :, 0, :]


def _lane_tiles_store(scr, v):
    for c in range(v.shape[1] // LANE_TILE):
        scr[c] = v[:, c * LANE_TILE:(c + 1) * LANE_TILE]


def _lane_tiles_load(scr, rows):
    return jnp.concatenate([scr[c, rows, :] for c in range(scr.shape[0])], axis=1)


def _ssm_sample_body(s_ref, h0_ref, bdb_ref, bdc_ref, tn_ref, tp_ref, l_ref, cm_ref, d_ref,
                     y_ref, h_ref, gre_scr, gim_scr, *, dec_seq):
    ks = KT_STATE
    nb = SSM_CHUNK // dec_seq
    cm = cm_ref[0]
    h0 = h0_ref[0]
    s_c = s_ref[...]
    g_re, g_im = _ssm_scaled_inputs(s_c, bdb_ref[0], tn_ref[0])
    c_re, c_im = _complex_mul(cm[:, :ks], cm[:, ks:], h0[:, :ks], h0[:, ks:])
    first = pl.ds(0, nb, stride=dec_seq)
    _lane_tiles_store(gre_scr, g_re)
    _lane_tiles_store(gim_scr, g_im)
    for c in range(ks // LANE_TILE):
        cols = slice(c * LANE_TILE, (c + 1) * LANE_TILE)
        gre_scr[c, first, :] = gre_scr[c, first, :] + c_re[:, cols]
        gim_scr[c, first, :] = gim_scr[c, first, :] + c_im[:, cols]
    l_mat = l_ref[...]
    p_re = _prefix_matmul(l_mat, _lane_tiles_load(gre_scr, slice(None)))
    p_im = _prefix_matmul(l_mat, _lane_tiles_load(gim_scr, slice(None)))
    y, h_re, h_im = _ssm_outputs(s_c, p_re, p_im, tp_ref[0], bdc_ref[0], d_ref[...])
    y_ref[...] = y
    _lane_tiles_store(gre_scr, h_re)
    _lane_tiles_store(gim_scr, h_im)
    last = pl.ds(dec_seq - 1, nb, stride=dec_seq)
    h_ref[0, :, :ks] = _lane_tiles_load(gre_scr, last)
    h_ref[0, :, ks:] = _lane_tiles_load(gim_scr, last)


def _ssm_core_sample(s, h0, prep, dec_seq):
    rows, d = s.shape
    t = SSM_CHUNK
    nb = t // dec_seq
    w2 = 2 * KT_STATE
    return pl.pallas_call(
        functools.partial(_ssm_sample_body, dec_seq=dec_seq),
        grid=(SSM_KT, rows // t),
        in_specs=[pl.BlockSpec((t, LANE_TILE), lambda k, c: (c, k)),
                  pl.BlockSpec((1, nb, w2), lambda k, c: (k, c, 0)),
                  pl.BlockSpec((1, LANE_TILE, w2), lambda k, c: (k, 0, 0)),
                  pl.BlockSpec((1, w2, LANE_TILE), lambda k, c: (k, 0, 0)),
                  pl.BlockSpec((1, t, w2), lambda k, c: (k, 0, 0)),
                  pl.BlockSpec((1, t, w2), lambda k, c: (k, 0, 0)),
                  pl.BlockSpec((t, t), lambda k, c: (0, 0)),
                  pl.BlockSpec((1, 1, w2), lambda k, c: (k, 0, 0)),
                  pl.BlockSpec((1, LANE_TILE), lambda k, c: (0, k))],
        out_specs=[pl.BlockSpec((t, LANE_TILE), lambda k, c: (c, k)),
                   pl.BlockSpec((1, nb, w2), lambda k, c: (k, c, 0))],
        out_shape=[jax.ShapeDtypeStruct((rows, d), F32),
                   jax.ShapeDtypeStruct((SSM_KT, rows // dec_seq, w2), F32)],
        scratch_shapes=[pltpu.VMEM((KT_STATE // LANE_TILE, t, LANE_TILE), F32)] * 2,
        compiler_params=_cparams(("arbitrary", "arbitrary")),
        name="ssm_core_sample",
    )(s, h0, prep["bdb"], prep["bdc"], prep["tn_s"], prep["tp_s"], prep["l_s"], prep["cm_s"],
      prep["dsk"])


def _complex_powers(ar, ai, kmax):
    def mul(x, y):
        return _complex_mul(x[0], x[1], y[0], y[1])

    reps = (jnp.broadcast_to(ar, (kmax,) + ar.shape), jnp.broadcast_to(ai, (kmax,) + ai.shape))
    pr, pi = lax.associative_scan(mul, reps, axis=0)
    one = jnp.ones((1,) + ar.shape, ar.dtype)
    return jnp.concatenate([one, pr], 0), jnp.concatenate([jnp.zeros_like(one), pi], 0)


def _to_kt_lanes(re, im):
    rows = re.shape[0]

    def tile(v):
        return v.reshape(rows, SSM_KT, KT_STATE).transpose(1, 0, 2)

    return jnp.concatenate([tile(re), tile(im)], axis=-1)


def _ssm_prepare(a_re, a_im, log_dt, b_re, b_im, c_re, c_im, d_skip, dec_seq):
    dt = jnp.exp(log_dt)[:, None]
    lr, li = a_re, a_im
    mag = jnp.exp(lr * dt)
    abar_re = mag * jnp.cos(li * dt)
    abar_im = mag * jnp.sin(li * dt)
    den = lr * lr + li * li
    xr = abar_re - 1.0
    g_re = (xr * lr + abar_im * li) / den
    g_im = (abar_im * lr - xr * li) / den
    bbar_re = g_re[..., None] * b_re - g_im[..., None] * b_im
    bbar_im = g_re[..., None] * b_im + g_im[..., None] * b_re

    eye = jnp.eye(GROUPS_PER_KT, dtype=a_re.dtype)

    def in_map(bb):
        bb = bb.reshape(SSM_KT, GROUPS_PER_KT, SSM_STATE, SSM_GROUP)
        m = jnp.einsum("ab,kbpc->kacbp", eye, bb)
        return m.reshape(SSM_KT, LANE_TILE, KT_STATE)

    def out_map(cc):
        cc = cc.reshape(SSM_KT, GROUPS_PER_KT, SSM_GROUP, SSM_STATE)
        m = jnp.einsum("ab,kbcp->kapbc", eye, cc)
        return m.reshape(SSM_KT, KT_STATE, LANE_TILE)

    bdb = jnp.concatenate([in_map(bbar_re), in_map(bbar_im)], axis=-1).astype(BF16)
    bdc = jnp.concatenate([out_map(c_re), out_map(-c_im)], axis=1).astype(BF16)

    t = SSM_CHUNK
    c = SSM_CENTER
    inv_den = abar_re * abar_re + abar_im * abar_im
    inv_re, inv_im = abar_re / inv_den, -abar_im / inv_den
    pos_re, pos_im = _complex_powers(abar_re, abar_im, c + 1)
    neg_re, neg_im = _complex_powers(inv_re, inv_im, c)

    def power_rows(exps):
        re = jnp.stack([pos_re[k] if k >= 0 else neg_re[-k] for k in exps], 0)
        im = jnp.stack([pos_im[k] if k >= 0 else neg_im[-k] for k in exps], 0)
        return _to_kt_lanes(re, im)

    tn_p = power_rows([c - j for j in range(t)])
    tp_p = power_rows([i - c for i in range(t)])
    cm_p = power_rows([c + 1])
    tn_s = power_rows([-(r % dec_seq) for r in range(t)])
    tp_s = power_rows([r % dec_seq for r in range(t)])
    cm_s = power_rows([1])

    idx = jnp.arange(t)
    l_p = (idx[:, None] >= idx[None, :]).astype(BF16)
    l_s = ((idx[:, None] >= idx[None, :])
           & (idx[:, None] // dec_seq == idx[None, :] // dec_seq)).astype(BF16)
    return dict(bdb=bdb, bdc=bdc, tn_p=tn_p, tp_p=tp_p, cm_p=cm_p, tn_s=tn_s, tp_s=tp_s,
                cm_s=cm_s, l_p=l_p, l_s=l_s, dsk=d_skip.reshape(1, D_MODEL))


def _state_to_kt(h_re, h_im):
    return _to_kt_lanes(h_re, h_im)


def _state_from_kt(h):
    bsz = h.shape[0]
    re = h[:, :, :KT_STATE].reshape(bsz, SSM_GROUPS, SSM_STATE)
    im = h[:, :, KT_STATE:].reshape(bsz, SSM_GROUPS, SSM_STATE)
    return re, im


ATT_MT = 2048
ATT_TN = 512
QKV_TILES = ATT_QKV // ATT_TN
TILES_PER_GROUP = QKV_TILES // 3
KV_FIRST_TILE = TILES_PER_GROUP // 3
KV_TILES = 2 * TILES_PER_GROUP // 3


def _attn_inproj_body(x_ref, sc_ref, sh_ref, w_ref, qkv_ref, kv0_ref, kv1_ref, kv2_ref,
                      u_scr, acc_scr, *, last_mt):
    mt = pl.program_id(1)
    nt = pl.program_id(2)

    @pl.when(nt == 0)
    def _():
        u_scr[...] = (x_ref[0] * (1.0 + sc_ref[0]) + sh_ref[0]).astype(BF16)

    _lane_tiles_store(acc_scr, jnp.dot(u_scr[...], w_ref[...], preferred_element_type=F32))
    n_ct = ATT_TN // LANE_TILE
    group = nt // TILES_PER_GROUP
    kv_refs = (kv0_ref, kv1_ref, kv2_ref)
    for g in range(3):
        win, dil = WINDOWS[g], DILATIONS[g]

        @pl.when(group == g)
        def _(win=win, dil=dil):
            for c in range(n_ct):
                cols = slice(c * LANE_TILE, (c + 1) * LANE_TILE)
                if dil == 1:
                    qkv_ref[0, :, cols] = acc_scr[c].astype(BF16)
                    continue
                for n in range(ATT_MT // win):
                    for r in range(dil):
                        src = acc_scr[c, pl.ds(n * win + r, BLK, stride=dil), :]
                        dst = n * win + r * BLK
                        qkv_ref[0, dst:dst + BLK, cols] = src.astype(BF16)

        first = g * TILES_PER_GROUP + KV_FIRST_TILE

        @pl.when((mt == last_mt) & (nt >= first) & (nt < first + KV_TILES))
        def _(win=win, kv_ref=kv_refs[g]):
            for c in range(n_ct):
                kv_ref[0, :, c * LANE_TILE:(c + 1) * LANE_TILE] = acc_scr[c, ATT_MT - win:, :]


def _attn_inproj_prompt(x, sc, sh, w_bf):
    bsz, length, d = x.shape
    n_mt = length // ATT_MT
    last_mt = n_mt - 1
    kv_width = KV_TILES * ATT_TN

    def kv_index(g):
        first = g * TILES_PER_GROUP + KV_FIRST_TILE

        def index(b, m, n):
            col = jnp.clip(n - first, 0, KV_TILES - 1)
            return (b, 0, jnp.where(m == last_mt, col, 0))

        return index

    return pl.pallas_call(
        functools.partial(_attn_inproj_body, last_mt=last_mt),
        grid=(bsz, n_mt, QKV_TILES),
        in_specs=[pl.BlockSpec((1, ATT_MT, d), lambda b, m, n: (b, m, 0)),
                  pl.BlockSpec((1, 1, d), lambda b, m, n: (b, 0, 0)),
                  pl.BlockSpec((1, 1, d), lambda b, m, n: (b, 0, 0)),
                  pl.BlockSpec((d, ATT_TN), lambda b, m, n: (0, n))],
        out_specs=[pl.BlockSpec((1, ATT_MT, ATT_TN), lambda b, m, n: (b, m, n))]
        + [pl.BlockSpec((1, WINDOWS[g], ATT_TN), kv_index(g)) for g in range(3)],
        out_shape=[jax.ShapeDtypeStruct((bsz, length, ATT_QKV), BF16)]
        + [jax.ShapeDtypeStruct((bsz, WINDOWS[g], kv_width), F32) for g in range(3)],
        scratch_shapes=[pltpu.VMEM((ATT_MT, d), BF16), pltpu.VMEM((ATT_TN // LANE_TILE, ATT_MT, LANE_TILE), F32)],
        compiler_params=_cparams(("arbitrary", "arbitrary", "arbitrary")),
        name="attn_inproj_prompt",
    )(x, sc, sh, w_bf)


def _attn_core_body(q0, k0, v0, q1, k1, v1, q2, k2, v2, bias_ref, o_ref, o_scr, l_scr,
                    *, length):
    qkv = ((q0, k0, v0), (q1, k1, v1), (q2, k2, v2))
    lane = lax.broadcasted_iota(jnp.int32, (1, LANE_TILE), 1)
    head_lanes = (lane < HEAD_DIM, lane >= HEAD_DIM)
    nt_dims = (((1,), (1,)), ((), ()))
    for g in range(3):
        win, dil = WINDOWS[g], DILATIONS[g]
        q_ref, k_ref, v_ref = qkv[g]
        for r in range(dil):
            def unit(n, carry, g=g, win=win, dil=dil, r=r, q_ref=q_ref, k_ref=k_ref, v_ref=v_ref):
                cur = pl.multiple_of(n * win + r * BLK, BLK)
                prev = pl.multiple_of(jnp.maximum(n - 1, 0) * win + r * BLK, BLK)
                q = q_ref[0, pl.ds(cur, BLK), :]
                k = jnp.concatenate([k_ref[0, pl.ds(prev, BLK), :], k_ref[0, pl.ds(cur, BLK), :]], 0)
                v = jnp.concatenate([v_ref[0, pl.ds(prev, BLK), :], v_ref[0, pl.ds(cur, BLK), :]], 0)
                variant = jnp.minimum(n, 1)
                outs, lses = [], []
                for hh in range(2):
                    qm = jnp.where(head_lanes[hh], q, jnp.zeros_like(q))
                    s = lax.dot_general(qm, k, nt_dims, preferred_element_type=F32) * QK_SCALE
                    s = s + bias_ref[g, variant, hh]
                    m = jnp.max(s, axis=-1, keepdims=True)
                    p = jnp.exp(s - m)
                    l = jnp.sum(p, axis=-1, keepdims=True)
                    outs.append(jnp.dot(p.astype(BF16), v, preferred_element_type=F32) / l)
                    lses.append(m + jnp.log(l))
                o = jnp.where(head_lanes[0], outs[0], outs[1])
                lse = jnp.where(head_lanes[0], lses[0], lses[1])
                rows = pl.ds(pl.multiple_of(n * win, BLK) + r, BLK, stride=dil)
                o_scr[g, rows, :] = o
                l_scr[g, rows, :] = lse
                return carry

            lax.fori_loop(0, length // win, unit, 0)

    merge_rows = 512
    for c in range(length // merge_rows):
        rows = slice(c * merge_rows, (c + 1) * merge_rows)
        l0, l1, l2 = l_scr[0, rows, :], l_scr[1, rows, :], l_scr[2, rows, :]
        m = jnp.maximum(jnp.maximum(l0, l1), l2)
        e0, e1, e2 = jnp.exp(l0 - m), jnp.exp(l1 - m), jnp.exp(l2 - m)
        num = e0 * o_scr[0, rows, :] + e1 * o_scr[1, rows, :] + e2 * o_scr[2, rows, :]
        o_ref[0, rows, :] = num / (e0 + e1 + e2)


def _prompt_bias():
    h = jnp.arange(ATT_HEADS, dtype=F32) + 1.0
    slopes = jnp.exp2(-ALIBI_BASE_EXP * h / ATT_HEADS)
    a = jnp.arange(BLK)[:, None]
    bk = jnp.arange(2 * BLK)[None, :]
    steps = a - bk + BLK
    ok = (steps >= 0) & (steps <= BLK)
    ok_first = ok & (bk >= BLK)
    tables = []
    for dil in DILATIONS:
        bias = -slopes[:, None, None] * (steps * dil).astype(F32)[None]
        tables.append(jnp.stack([jnp.where(ok_first[None], bias, -jnp.inf),
                                 jnp.where(ok[None], bias, -jnp.inf)], 0))
    return jnp.stack(tables, 0)


def _attn_core_prompt(qkv, bias):
    bsz, length, _ = qkv.shape
    n_hp = ATT_HEADS // 2
    col_tiles = ATT_HEADS * HEAD_DIM // LANE_TILE

    def col_spec(g, part):
        base = (g * 3 + part) * col_tiles
        return pl.BlockSpec((1, length, LANE_TILE), lambda b, hp: (b, 0, base + hp))

    in_specs = [col_spec(g, part) for g in range(3) for part in range(3)]
    in_specs.append(pl.BlockSpec((3, 2, 2, BLK, 2 * BLK), lambda b, hp: (0, 0, hp, 0, 0)))
    return pl.pallas_call(
        functools.partial(_attn_core_body, length=length),
        grid=(bsz, n_hp),
        in_specs=in_specs,
        out_specs=pl.BlockSpec((1, length, LANE_TILE), lambda b, hp: (b, 0, hp)),
        out_shape=jax.ShapeDtypeStruct((bsz, length, ATT_HEADS * HEAD_DIM), F32),
        scratch_shapes=[pltpu.VMEM((3, length, LANE_TILE), F32),
                        pltpu.VMEM((3, length, LANE_TILE), F32)],
        compiler_params=_cparams(("arbitrary", "arbitrary")),
        name="attn_core_prompt",
    )(*([qkv] * 9), bias)


SAMPLE_PHASES = 4


def _sample_attend(q, t, k_c, v_c, rho, k_n, v_n, slopes, win, dil):
    dist_c = (win + t).astype(F32) - rho
    s_c = jnp.sum(k_c * q[None], axis=-1, keepdims=True) * QK_SCALE - slopes * dist_c
    s_c = jnp.where(dist_c <= float(win), s_c, -jnp.inf)
    tn = lax.broadcasted_iota(jnp.int32, (k_n.shape[0], 1, 1), 0)
    diff = t - tn
    s_n = jnp.sum(k_n * q[None], axis=-1, keepdims=True) * QK_SCALE - slopes * diff.astype(F32)
    s_n = jnp.where((diff >= 0) & ((diff & (dil - 1)) == 0), s_n, -jnp.inf)
    m = jnp.maximum(jnp.max(s_c, axis=0, keepdims=True), jnp.max(s_n, axis=0, keepdims=True))
    p_c = jnp.exp(s_c - m)
    p_n = jnp.exp(s_n - m)
    l = jnp.sum(p_c, axis=0, keepdims=True) + jnp.sum(p_n, axis=0, keepdims=True)
    o = (jnp.sum(p_c * v_c, axis=0) + jnp.sum(p_n * v_n, axis=0)) / l[0]
    return o, (m + jnp.log(l))[0]


def _attn_sample_body(q_ref, new_ref, c0_ref, c1_ref, c2_ref, slope_ref, o_ref, o_scr, l_scr):
    p = pl.program_id(1)
    slopes = slope_ref[...]
    m_idx = lax.broadcasted_iota(jnp.int32, (BLK, 1, 1), 0).astype(F32)
    for g in range(3):
        win, dil = WINDOWS[g], DILATIONS[g]
        k_n = new_ref[g, 0, :, 0]
        v_n = new_ref[g, 0, :, 1]
        if g == 0:
            queries = [(2 * p, c0_ref[0, 0, :, 0], c0_ref[0, 0, :, 1], m_idx),
                       (2 * p + 1, c0_ref[0, 0, :, 0], c0_ref[0, 0, :, 1], m_idx)]
        elif g == 1:
            k_c, v_c = c1_ref[0, 0, :, 0, 0], c1_ref[0, 0, :, 0, 1]
            rho = p.astype(F32) + dil * m_idx
            queries = [(p, k_c, v_c, rho), (p + SAMPLE_PHASES, k_c, v_c, rho)]
        else:
            queries = []
            for i in range(2):
                t = 2 * p + i
                queries.append((t, c2_ref[0, 0, :, i, 0], c2_ref[0, 0, :, i, 1],
                                t.astype(F32) + dil * m_idx))
        for t, k_c, v_c, rho in queries:
            o, lse = _sample_attend(q_ref[g, 0, t], t, k_c, v_c, rho, k_n, v_n, slopes, win, dil)
            o_scr[g, t] = o
            l_scr[g, t] = lse

    @pl.when(p == SAMPLE_PHASES - 1)
    def _():
        l_all = l_scr[...]
        m = jnp.max(l_all, axis=0, keepdims=True)
        w = jnp.exp(l_all - m)
        o_ref[0] = jnp.sum(w * o_scr[...], axis=0) / jnp.sum(w, axis=0)


def _attn_core_sample(q, new_rows, caches, layer):
    _, dbsz, tlen, heads, hd = q.shape
    c0, c1, c2 = caches
    nl = c0.shape[0]
    c1 = c1.reshape(nl, dbsz, BLK, DILATIONS[1], 2, heads, hd)
    c2 = c2.reshape(nl, dbsz, BLK, DILATIONS[2], 2, heads, hd)
    h = jnp.arange(heads, dtype=F32) + 1.0
    slopes = jnp.exp2(-ALIBI_BASE_EXP * h / heads).reshape(1, heads, 1)
    return pl.pallas_call(
        _attn_sample_body,
        grid=(dbsz, SAMPLE_PHASES),
        in_specs=[pl.BlockSpec((3, 1, tlen, heads, hd), lambda b, p: (0, b, 0, 0, 0)),
                  pl.BlockSpec((3, 1, tlen, 2, heads, hd), lambda b, p: (0, b, 0, 0, 0, 0)),
                  pl.BlockSpec((1, 1, BLK, 2, heads, hd), lambda b, p: (layer, b, 0, 0, 0, 0)),
                  pl.BlockSpec((1, 1, BLK, 1, 2, heads, hd),
                               lambda b, p: (layer, b, 0, p, 0, 0, 0)),
                  pl.BlockSpec((1, 1, BLK, 2, 2, heads, hd),
                               lambda b, p: (layer, b, 0, p, 0, 0, 0)),
                  pl.BlockSpec((1, heads, 1), lambda b, p: (0, 0, 0))],
        out_specs=pl.BlockSpec((1, tlen, heads, hd), lambda b, p: (b, 0, 0, 0)),
        out_shape=jax.ShapeDtypeStruct((dbsz, tlen, heads, hd), F32),
        scratch_shapes=[pltpu.VMEM((3, tlen, heads, hd), F32),
                        pltpu.VMEM((3, tlen, heads, 1), F32)],
        compiler_params=_cparams(("arbitrary", "arbitrary")),
        name="attn_core_sample",
    )(q, new_rows, c0, c1, c2, slopes)


def kernel(x_prompt, x_sample, c_prompt, c_sample, state_ssm_re, state_ssm_im, cache_kv_w128, cache_kv_w512, cache_kv_w2048, w_ada, b_ada, ln_g, ln_b, ssm_w_in, ssm_a_re, ssm_a_im, ssm_log_dt, ssm_b_re, ssm_b_im, ssm_c_re, ssm_c_im, ssm_d, ssm_w_glu, ssm_b_glu, ssm_w_out, attn_w_in, attn_w_out):
    bsz, seq, d = x_prompt.shape
    dbsz, dec_seq, _ = x_sample.shape
    n_dec = dbsz * dec_seq
    att_width = ATT_HEADS * HEAD_DIM

    pad = (-(bsz + dbsz)) % 8
    c_all = jnp.concatenate([c_prompt, c_sample, jnp.zeros((pad, d), c_prompt.dtype)], 0)
    mod = _ada(c_all, w_ada, b_ada)
    bias_p = _prompt_bias()

    xp = x_prompt
    xs = x_sample.reshape(1, n_dec, d)
    ssm_p_re, ssm_p_im, ssm_s_re, ssm_s_im = [], [], [], []
    kv_p = [[], [], []]
    kv_s = [[], [], []]
    for i in range(DEPTH):
        j = i // 2
        mod_p = mod[i, :bsz]
        mod_s = jnp.repeat(mod[i, bsz:bsz + dbsz], dec_seq, axis=0)
        sh_p, sc_p, gt_p = [mod_p[:, None, k * d:(k + 1) * d] for k in range(3)]
        sh_s, sc_s, gt_s = [mod_s[None, :, k * d:(k + 1) * d] for k in range(3)]
        if i % 2 == 0:
            prep = _ssm_prepare(ssm_a_re[j], ssm_a_im[j], ssm_log_dt[j], ssm_b_re[j], ssm_b_im[j],
                                ssm_c_re[j], ssm_c_im[j], ssm_d[j], dec_seq)
            w_in = ssm_w_in[j].astype(BF16)
            w_s, w_z = w_in[:, :d], w_in[:, d:]
            w_glu = ssm_w_glu[j].astype(BF16)
            w_out = ssm_w_out[j].astype(BF16)

            s_p = _modmm(xp, sc_p, sh_p, w_s, 1024, d)
            y_p, h_p = _ssm_core_prompt(s_p, prep, 512)
            hr, hi = _state_from_kt(h_p)
            ssm_p_re.append(hr)
            ssm_p_im.append(hi)
            xp = _tail(xp, sc_p, sh_p, gt_p, y_p, w_z, w_out, ln_g[i], ln_b[i], 512,
                       wglu_bf=w_glu, bglu=ssm_b_glu[j])

            s_s = _modmm(xs, sc_s, sh_s, w_s, n_dec, d)
            h0 = _state_to_kt(state_ssm_re[j], state_ssm_im[j])
            y_s, h_s = _ssm_core_sample(s_s[0], h0, prep, dec_seq)
            hr, hi = _state_from_kt(h_s.transpose(1, 0, 2))
            ssm_s_re.append(hr)
            ssm_s_im.append(hi)
            xs = _tail(xs, sc_s, sh_s, gt_s, y_s[None], w_z, w_out, ln_g[i], ln_b[i], 512,
                       wglu_bf=w_glu, bglu=ssm_b_glu[j])
        else:
            w_in = attn_w_in[j].astype(BF16)
            w_qkv, w_z = w_in[:, :ATT_QKV], w_in[:, ATT_QKV:]
            w_out = attn_w_out[j].astype(BF16)

            qkv, kv0, kv1, kv2 = _attn_inproj_prompt(xp, sc_p, sh_p, w_qkv)
            for g, kv in enumerate((kv0, kv1, kv2)):
                kv_p[g].append(kv.reshape(bsz, WINDOWS[g], 2, ATT_HEADS, HEAD_DIM))
            o_p = _attn_core_prompt(qkv, bias_p)
            xp = _tail(xp, sc_p, sh_p, gt_p, o_p, w_z, w_out, ln_g[i], ln_b[i], 512)

            proj_s = _modmm(xs, sc_s, sh_s, w_qkv, n_dec, 1024)[0]
            proj_s = proj_s.reshape(dbsz, dec_seq, 3, 3, ATT_HEADS, HEAD_DIM)
            q_s = proj_s[:, :, :, 0].transpose(2, 0, 1, 3, 4)
            new_s = proj_s[:, :, :, 1:].transpose(2, 0, 1, 3, 4, 5)
            for g in range(3):
                kv_s[g].append(new_s[g])
            o_s = _attn_core_sample(q_s, new_s, (cache_kv_w128, cache_kv_w512, cache_kv_w2048), j)
            xs = _tail(xs, sc_s, sh_s, gt_s, o_s.reshape(1, n_dec, att_width), w_z, w_out,
                       ln_g[i], ln_b[i], 512)

    return (xp, xs.reshape(dbsz, dec_seq, d),
            jnp.stack(ssm_p_re, 0), jnp.stack(ssm_p_im, 0),
            jnp.stack(kv_p[0], 0), jnp.stack(kv_p[1], 0), jnp.stack(kv_p[2], 0),
            jnp.stack(ssm_s_re, 0), jnp.stack(ssm_s_im, 0),
            jnp.stack(kv_s[0], 0), jnp.stack(kv_s[1], 0), jnp.stack(kv_s[2], 0))
```

```python
import functools

import numpy as np
import jax
import jax.numpy as jnp
from jax import lax
from jax.experimental import pallas as pl
from jax.experimental.pallas import tpu as pltpu

F32 = jnp.float32
BF16 = jnp.bfloat16

D_MODEL = 1024
DEPTH = 4
SSM_GROUPS = 64
SSM_GROUP = 16
SSM_STATE = 64
ATT_HEADS = 16
HEAD_DIM = 64
WINDOWS = (128, 512, 2048)
DILATIONS = (1, 4, 16)
ATT_QKV = 3 * 3 * ATT_HEADS * HEAD_DIM
ALIBI_BASE_EXP = 8.0
DEEPNORM_ALPHA = (2 * DEPTH) ** 0.25
LN_EPS = 1e-5
QK_SCALE = HEAD_DIM ** -0.5

LANE_TILE = 128
BLK = 128
SSM_CHUNK = 128
SSM_CENTER = SSM_CHUNK // 2
SSM_KT = D_MODEL // LANE_TILE
GROUPS_PER_KT = LANE_TILE // SSM_GROUP
KT_STATE = GROUPS_PER_KT * SSM_STATE
VMEM_LIMIT = 56 * 1024 * 1024


def _cparams(sem):
    return pltpu.CompilerParams(dimension_semantics=sem, vmem_limit_bytes=VMEM_LIMIT)


def _ada_body(c_ref, w_ref, b_ref, o_ref):
    c = c_ref[...]
    s = c * jax.nn.sigmoid(c)
    o_ref[0] = jnp.dot(s.astype(BF16), w_ref[0].astype(BF16), preferred_element_type=F32) + b_ref[0]


def _ada(c_all, w_ada, b_ada):
    rows = c_all.shape[0]
    d = D_MODEL
    return pl.pallas_call(
        _ada_body,
        grid=(DEPTH, 3),
        in_specs=[pl.BlockSpec((rows, d), lambda i, n: (0, 0)),
                  pl.BlockSpec((1, d, d), lambda i, n: (i, 0, n)),
                  pl.BlockSpec((1, 1, d), lambda i, n: (i, 0, n))],
        out_specs=pl.BlockSpec((1, rows, d), lambda i, n: (i, 0, n)),
        out_shape=jax.ShapeDtypeStruct((DEPTH, rows, 3 * d), F32),
        compiler_params=_cparams(("arbitrary", "arbitrary")),
        name="ada",
    )(c_all, w_ada, b_ada.reshape(DEPTH, 1, 3 * d))


def _modmm_body(x_ref, sc_ref, sh_ref, w_ref, o_ref, u_scr):
    @pl.when(pl.program_id(2) == 0)
    def _():
        u_scr[...] = (x_ref[0] * (1.0 + sc_ref[0]) + sh_ref[0]).astype(BF16)

    o_ref[0] = jnp.dot(u_scr[...], w_ref[...], preferred_element_type=F32)


def _mod_index(per_token):
    if per_token:
        return lambda b, m, n: (b, m, 0)
    return lambda b, m, n: (b, 0, 0)


def _modmm(x, sc, sh, w_bf, tm, tn):
    bsz, length, d = x.shape
    n_out = w_bf.shape[1]
    per_token = sc.shape[1] != 1
    mod_rows = tm if per_token else 1
    return pl.pallas_call(
        _modmm_body,
        grid=(bsz, length // tm, n_out // tn),
        in_specs=[pl.BlockSpec((1, tm, d), lambda b, m, n: (b, m, 0)),
                  pl.BlockSpec((1, mod_rows, d), _mod_index(per_token)),
                  pl.BlockSpec((1, mod_rows, d), _mod_index(per_token)),
                  pl.BlockSpec((d, tn), lambda b, m, n: (0, n))],
        out_specs=pl.BlockSpec((1, tm, tn), lambda b, m, n: (b, m, n)),
        out_shape=jax.ShapeDtypeStruct((bsz, length, n_out), F32),
        scratch_shapes=[pltpu.VMEM((tm, d), BF16)],
        compiler_params=_cparams(("arbitrary", "arbitrary", "arbitrary")),
        name="mod_proj",
    )(x, sc, sh, w_bf)


def _tail_body(*refs, glu):
    if glu:
        (x_ref, sc_ref, sh_ref, gt_ref, y_ref, wz_ref, wglu_ref, bglu_ref, wout_ref,
         lng_ref, lnb_ref, o_ref) = refs
    else:
        (x_ref, sc_ref, sh_ref, gt_ref, y_ref, wz_ref, wout_ref, lng_ref, lnb_ref, o_ref) = refs
    x = x_ref[0]
    u = (x * (1.0 + sc_ref[0]) + sh_ref[0]).astype(BF16)
    z = jnp.dot(u, wz_ref[...], preferred_element_type=F32)
    y = y_ref[0]
    if glu:
        y = y * jax.nn.sigmoid(
            jnp.dot(y.astype(BF16), wglu_ref[...], preferred_element_type=F32) + bglu_ref[...])
    y = y * (z * jax.nn.sigmoid(z))
    out = jnp.dot(y.astype(BF16), wout_ref[...], preferred_element_type=F32)
    r = DEEPNORM_ALPHA * x + gt_ref[0] * out
    mu = jnp.mean(r, axis=-1, keepdims=True)
    rc = r - mu
    var = jnp.mean(rc * rc, axis=-1, keepdims=True)
    o_ref[0] = rc * lax.rsqrt(var + LN_EPS) * lng_ref[...] + lnb_ref[...]


def _tail(x, sc, sh, gt, y, wz_bf, wout_bf, ln_g, ln_b, tm, wglu_bf=None, bglu=None):
    bsz, length, d = x.shape
    per_token = sc.shape[1] != 1
    mod_rows = tm if per_token else 1
    if per_token:
        mod_idx = lambda b, m: (b, m, 0)
    else:
        mod_idx = lambda b, m: (b, 0, 0)
    row_spec = pl.BlockSpec((1, tm, d), lambda b, m: (b, m, 0))
    mod_spec = pl.BlockSpec((1, mod_rows, d), mod_idx)
    w_spec = pl.BlockSpec((d, d), lambda b, m: (0, 0))
    vec_spec = pl.BlockSpec((1, d), lambda b, m: (0, 0))
    glu = wglu_bf is not None
    operands = [x, sc, sh, gt, y, wz_bf]
    in_specs = [row_spec, mod_spec, mod_spec, mod_spec, row_spec, w_spec]
    if glu:
        operands += [wglu_bf, bglu.reshape(1, d)]
        in_specs += [w_spec, vec_spec]
    operands += [wout_bf, ln_g.reshape(1, d), ln_b.reshape(1, d)]
    in_specs += [w_spec, vec_spec, vec_spec]
    return pl.pallas_call(
        functools.partial(_tail_body, glu=glu),
        grid=(bsz, length // tm),
        in_specs=in_specs,
        out_specs=row_spec,
        out_shape=jax.ShapeDtypeStruct((bsz, length, d), F32),
        compiler_params=_cparams(("arbitrary", "arbitrary")),
        name="tail_glu" if glu else "tail",
    )(*operands)


def _complex_mul(ar, ai, br, bi):
    return ar * br - ai * bi, ar * bi + ai * br


def _prefix_matmul(l_mat, g):
    hi = g.astype(BF16)
    lo = (g - hi.astype(F32)).astype(BF16)
    return (jnp.dot(l_mat, hi, preferred_element_type=F32)
            + jnp.dot(l_mat, lo, preferred_element_type=F32))


def _ssm_scaled_inputs(s_c, bdb, tn):
    ks = KT_STATE
    bu = jnp.dot(s_c.astype(BF16), bdb, preferred_element_type=F32)
    return _complex_mul(tn[:, :ks], tn[:, ks:], bu[:, :ks], bu[:, ks:])


def _ssm_outputs(s_c, p_re, p_im, tp, bdc, dsk):
    ks = KT_STATE
    h_re, h_im = _complex_mul(tp[:, :ks], tp[:, ks:], p_re, p_im)
    y = (jnp.dot(h_re.astype(BF16), bdc[:ks], preferred_element_type=F32)
         + jnp.dot(h_im.astype(BF16), bdc[ks:], preferred_element_type=F32))
    return jax.nn.gelu(y + dsk * s_c), h_re, h_im


def _ssm_prompt_body(s_ref, bdb_ref, bdc_ref, tn_ref, tp_ref, l_ref, cm_ref, d_ref,
                     y_ref, h_ref, carry_scr, *, n_chunks):
    ks = KT_STATE
    t = SSM_CHUNK

    @pl.when(pl.program_id(2) == 0)
    def _():
        carry_scr[...] = jnp.zeros_like(carry_scr)

    l_mat = l_ref[...]
    tn = tn_ref[0]
    tp = tp_ref[0]
    cm = cm_ref[0]
    bdb = bdb_ref[0]
    bdc = bdc_ref[0]
    dsk = d_ref[...]
    row0 = lax.broadcasted_iota(jnp.int32, (t, 1), 0) == 0
    hp = carry_scr[0:1, :]
    for ci in range(n_chunks):
        s_c = s_ref[0, ci * t:(ci + 1) * t, :]
        g_re, g_im = _ssm_scaled_inputs(s_c, bdb, tn)
        c_re, c_im = _complex_mul(cm[:, :ks], cm[:, ks:], hp[:, :ks], hp[:, ks:])
        g_re = jnp.where(row0, g_re + c_re, g_re)
        g_im = jnp.where(row0, g_im + c_im, g_im)
        p_re = _prefix_matmul(l_mat, g_re)
        p_im = _prefix_matmul(l_mat, g_im)
        y, h_re, h_im = _ssm_outputs(s_c, p_re, p_im, tp, bdc, dsk)
        y_ref[0, ci * t:(ci + 1) * t, :] = y
        hp = jnp.concatenate([h_re[t - 1:t], h_im[t - 1:t]], axis=1)
    carry_scr[0:1, :] = hp
    h_ref[0, 0] = jnp.broadcast_to(hp, (8, 2 * ks))


def _ssm_core_prompt(s, prep, tt):
    bsz, length, d = s.shape
    t = SSM_CHUNK
    w2 = 2 * KT_STATE
    y, h = pl.pallas_call(
        functools.partial(_ssm_prompt_body, n_chunks=tt // t),
        grid=(bsz, SSM_KT, length // tt),
        in_specs=[pl.BlockSpec((1, tt, LANE_TILE), lambda b, k, c: (b, c, k)),
                  pl.BlockSpec((1, LANE_TILE, w2), lambda b, k, c: (k, 0, 0)),
                  pl.BlockSpec((1, w2, LANE_TILE), lambda b, k, c: (k, 0, 0)),
                  pl.BlockSpec((1, t, w2), lambda b, k, c: (k, 0, 0)),
                  pl.BlockSpec((1, t, w2), lambda b, k, c: (k, 0, 0)),
                  pl.BlockSpec((t, t), lambda b, k, c: (0, 0)),
                  pl.BlockSpec((1, 1, w2), lambda b, k, c: (k, 0, 0)),
                  pl.BlockSpec((1, LANE_TILE), lambda b, k, c: (0, k))],
        out_specs=[pl.BlockSpec((1, tt, LANE_TILE), lambda b, k, c: (b, c, k)),
                   pl.BlockSpec((1, 1, 8, w2), lambda b, k, c: (b, k, 0, 0))],
        out_shape=[jax.ShapeDtypeStruct((bsz, length, d), F32),
                   jax.ShapeDtypeStruct((bsz, SSM_KT, 8, w2), F32)],
        scratch_shapes=[pltpu.VMEM((8, w2), F32)],
        compiler_params=_cparams(("arbitrary", "arbitrary", "arbitrary")),
        name="ssm_core_prompt",
    )(s, prep["bdb"], prep["bdc"], prep["tn_p"], prep["tp_p"], prep["l_p"], prep["cm_p"],
      prep["dsk"])
    return y, h[:, :, 0, :]


def _lane_tiles_store(scr, v):
    for c in range(v.shape[1] // LANE_TILE):
        scr[c] = v[:, c * LANE_TILE:(c + 1) * LANE_TILE]


def _lane_tiles_load(scr, rows):
    return jnp.concatenate([scr[c, rows, :] for c in range(scr.shape[0])], axis=1)


def _ssm_sample_body(s_ref, h0_ref, bdb_ref, bdc_ref, tn_ref, tp_ref, l_ref, cm_ref, d_ref,
                     y_ref, h_ref, gre_scr, gim_scr, *, dec_seq):
    ks = KT_STATE
    nb = SSM_CHUNK // dec_seq
    cm = cm_ref[0]
    h0 = h0_ref[0]
    s_c = s_ref[...]
    g_re, g_im = _ssm_scaled_inputs(s_c, bdb_ref[0], tn_ref[0])
    c_re, c_im = _complex_mul(cm[:, :ks], cm[:, ks:], h0[:, :ks], h0[:, ks:])
    first = pl.ds(0, nb, stride=dec_seq)
    _lane_tiles_store(gre_scr, g_re)
    _lane_tiles_store(gim_scr, g_im)
    for c in range(ks // LANE_TILE):
        cols = slice(c * LANE_TILE, (c + 1) * LANE_TILE)
        gre_scr[c, first, :] = gre_scr[c, first, :] + c_re[:, cols]
        gim_scr[c, first, :] = gim_scr[c, first, :] + c_im[:, cols]
    l_mat = l_ref[...]
    p_re = _prefix_matmul(l_mat, _lane_tiles_load(gre_scr, slice(None)))
    p_im = _prefix_matmul(l_mat, _lane_tiles_load(gim_scr, slice(None)))
    y, h_re, h_im = _ssm_outputs(s_c, p_re, p_im, tp_ref[0], bdc_ref[0], d_ref[...])
    y_ref[...] = y
    _lane_tiles_store(gre_scr, h_re)
    _lane_tiles_store(gim_scr, h_im)
    last = pl.ds(dec_seq - 1, nb, stride=dec_seq)
    h_ref[0, :, :ks] = _lane_tiles_load(gre_scr, last)
    h_ref[0, :, ks:] = _lane_tiles_load(gim_scr, last)


def _ssm_core_sample(s, h0, prep, dec_seq):
    rows, d = s.shape
    t = SSM_CHUNK
    nb = t // dec_seq
    w2 = 2 * KT_STATE
    return pl.pallas_call(
        functools.partial(_ssm_sample_body, dec_seq=dec_seq),
        grid=(SSM_KT, rows // t),
        in_specs=[pl.BlockSpec((t, LANE_TILE), lambda k, c: (c, k)),
                  pl.BlockSpec((1, nb, w2), lambda k, c: (k, c, 0)),
                  pl.BlockSpec((1, LANE_TILE, w2), lambda k, c: (k, 0, 0)),
                  pl.BlockSpec((1, w2, LANE_TILE), lambda k, c: (k, 0, 0)),
                  pl.BlockSpec((1, t, w2), lambda k, c: (k, 0, 0)),
                  pl.BlockSpec((1, t, w2), lambda k, c: (k, 0, 0)),
                  pl.BlockSpec((t, t), lambda k, c: (0, 0)),
                  pl.BlockSpec((1, 1, w2), lambda k, c: (k, 0, 0)),
                  pl.BlockSpec((1, LANE_TILE), lambda k, c: (0, k))],
        out_specs=[pl.BlockSpec((t, LANE_TILE), lambda k, c: (c, k)),
                   pl.BlockSpec((1, nb, w2), lambda k, c: (k, c, 0))],
        out_shape=[jax.ShapeDtypeStruct((rows, d), F32),
                   jax.ShapeDtypeStruct((SSM_KT, rows // dec_seq, w2), F32)],
        scratch_shapes=[pltpu.VMEM((KT_STATE // LANE_TILE, t, LANE_TILE), F32)] * 2,
        compiler_params=_cparams(("arbitrary", "arbitrary")),
        name="ssm_core_sample",
    )(s, h0, prep["bdb"], prep["bdc"], prep["tn_s"], prep["tp_s"], prep["l_s"], prep["cm_s"],
      prep["dsk"])


def _to_kt_lanes(re, im):
    rows = re.shape[0]

    def tile(v):
        return v.reshape(rows, SSM_KT, KT_STATE).transpose(1, 0, 2)

    return jnp.concatenate([tile(re), tile(im)], axis=-1)


def _ssm_prepare(a_re, a_im, log_dt, b_re, b_im, c_re, c_im, d_skip, dec_seq):
    dt = jnp.exp(log_dt)[:, None]
    lr, li = a_re, a_im
    mag = jnp.exp(lr * dt)
    abar_re = mag * jnp.cos(li * dt)
    abar_im = mag * jnp.sin(li * dt)
    den = lr * lr + li * li
    xr = abar_re - 1.0
    g_re = (xr * lr + abar_im * li) / den
    g_im = (abar_im * lr - xr * li) / den
    bbar_re = g_re[..., None] * b_re - g_im[..., None] * b_im
    bbar_im = g_re[..., None] * b_im + g_im[..., None] * b_re

    eye = jnp.eye(GROUPS_PER_KT, dtype=a_re.dtype)

    def in_map(bb):
        bb = bb.reshape(SSM_KT, GROUPS_PER_KT, SSM_STATE, SSM_GROUP)
        m = jnp.einsum("ab,kbpc->kacbp", eye, bb)
        return m.reshape(SSM_KT, LANE_TILE, KT_STATE)

    def out_map(cc):
        cc = cc.reshape(SSM_KT, GROUPS_PER_KT, SSM_GROUP, SSM_STATE)
        m = jnp.einsum("ab,kbcp->kapbc", eye, cc)
        return m.reshape(SSM_KT, KT_STATE, LANE_TILE)

    bdb = jnp.concatenate([in_map(bbar_re), in_map(bbar_im)], axis=-1).astype(BF16)
    bdc = jnp.concatenate([out_map(c_re), out_map(-c_im)], axis=1).astype(BF16)

    t = SSM_CHUNK
    c = SSM_CENTER
    log_mag = (lr * dt).reshape(SSM_KT, 1, KT_STATE)
    phase = (li * dt).reshape(SSM_KT, 1, KT_STATE)

    def power_rows(exps):
        k = jnp.asarray(exps, dtype=a_re.dtype).reshape(1, len(exps), 1)
        mag_k = jnp.exp(k * log_mag)
        return jnp.concatenate([mag_k * jnp.cos(k * phase), mag_k * jnp.sin(k * phase)], axis=-1)

    tn_p = power_rows([c - j for j in range(t)])
    tp_p = power_rows([i - c for i in range(t)])
    cm_p = power_rows([c + 1])
    tn_s = power_rows([-(r % dec_seq) for r in range(t)])
    tp_s = power_rows([r % dec_seq for r in range(t)])
    cm_s = power_rows([1])

    idx = jnp.arange(t)
    l_p = (idx[:, None] >= idx[None, :]).astype(BF16)
    l_s = ((idx[:, None] >= idx[None, :])
           & (idx[:, None] // dec_seq == idx[None, :] // dec_seq)).astype(BF16)
    return dict(bdb=bdb, bdc=bdc, tn_p=tn_p, tp_p=tp_p, cm_p=cm_p, tn_s=tn_s, tp_s=tp_s,
                cm_s=cm_s, l_p=l_p, l_s=l_s, dsk=d_skip.reshape(1, D_MODEL))


def _state_to_kt(h_re, h_im):
    return _to_kt_lanes(h_re, h_im)


def _state_from_kt(h):
    bsz = h.shape[0]
    re = h[:, :, :KT_STATE].reshape(bsz, SSM_GROUPS, SSM_STATE)
    im = h[:, :, KT_STATE:].reshape(bsz, SSM_GROUPS, SSM_STATE)
    return re, im


ATT_MT = 2048
ATT_TN = 512
QKV_TILES = ATT_QKV // ATT_TN
TILES_PER_GROUP = QKV_TILES // 3
KV_FIRST_TILE = TILES_PER_GROUP // 3
KV_TILES = 2 * TILES_PER_GROUP // 3
UNIT_UNROLL = 4


def _attn_inproj_body(x_ref, sc_ref, sh_ref, w_ref, qkv_ref, kv0_ref, kv1_ref, kv2_ref,
                      u_scr, acc_scr, *, last_mt):
    mt = pl.program_id(1)
    nt = pl.program_id(2)

    @pl.when(nt == 0)
    def _():
        u_scr[...] = (x_ref[0] * (1.0 + sc_ref[0]) + sh_ref[0]).astype(BF16)

    _lane_tiles_store(acc_scr, jnp.dot(u_scr[...], w_ref[...], preferred_element_type=F32))
    n_ct = ATT_TN // LANE_TILE
    group = nt // TILES_PER_GROUP
    kv_refs = (kv0_ref, kv1_ref, kv2_ref)
    for g in range(3):
        win, dil = WINDOWS[g], DILATIONS[g]

        @pl.when(group == g)
        def _(win=win, dil=dil):
            for c in range(n_ct):
                cols = slice(c * LANE_TILE, (c + 1) * LANE_TILE)
                if dil == 1:
                    qkv_ref[0, :, cols] = acc_scr[c].astype(BF16)
                    continue
                for n in range(ATT_MT // win):
                    for r in range(dil):
                        src = acc_scr[c, pl.ds(n * win + r, BLK, stride=dil), :]
                        dst = n * win + r * BLK
                        qkv_ref[0, dst:dst + BLK, cols] = src.astype(BF16)

        first = g * TILES_PER_GROUP + KV_FIRST_TILE

        @pl.when((mt == last_mt) & (nt >= first) & (nt < first + KV_TILES))
        def _(win=win, kv_ref=kv_refs[g]):
            for c in range(n_ct):
                kv_ref[0, :, c * LANE_TILE:(c + 1) * LANE_TILE] = acc_scr[c, ATT_MT - win:, :]


def _attn_inproj_prompt(x, sc, sh, w_bf):
    bsz, length, d = x.shape
    n_mt = length // ATT_MT
    last_mt = n_mt - 1
    kv_width = KV_TILES * ATT_TN

    def kv_index(g):
        first = g * TILES_PER_GROUP + KV_FIRST_TILE

        def index(b, m, n):
            col = jnp.clip(n - first, 0, KV_TILES - 1)
            return (b, 0, jnp.where(m == last_mt, col, 0))

        return index

    return pl.pallas_call(
        functools.partial(_attn_inproj_body, last_mt=last_mt),
        grid=(bsz, n_mt, QKV_TILES),
        in_specs=[pl.BlockSpec((1, ATT_MT, d), lambda b, m, n: (b, m, 0)),
                  pl.BlockSpec((1, 1, d), lambda b, m, n: (b, 0, 0)),
                  pl.BlockSpec((1, 1, d), lambda b, m, n: (b, 0, 0)),
                  pl.BlockSpec((d, ATT_TN), lambda b, m, n: (0, n))],
        out_specs=[pl.BlockSpec((1, ATT_MT, ATT_TN), lambda b, m, n: (b, m, n))]
        + [pl.BlockSpec((1, WINDOWS[g], ATT_TN), kv_index(g)) for g in range(3)],
        out_shape=[jax.ShapeDtypeStruct((bsz, length, ATT_QKV), BF16)]
        + [jax.ShapeDtypeStruct((bsz, WINDOWS[g], kv_width), F32) for g in range(3)],
        scratch_shapes=[pltpu.VMEM((ATT_MT, d), BF16), pltpu.VMEM((ATT_TN // LANE_TILE, ATT_MT, LANE_TILE), F32)],
        compiler_params=_cparams(("arbitrary", "arbitrary", "arbitrary")),
        name="attn_inproj_prompt",
    )(x, sc, sh, w_bf)


def _attn_core_body(q0, k0, v0, q1, k1, v1, q2, k2, v2, bias_ref, o_ref, o_scr, l_scr,
                    *, length):
    qkv = ((q0, k0, v0), (q1, k1, v1), (q2, k2, v2))
    lane = lax.broadcasted_iota(jnp.int32, (1, LANE_TILE), 1)
    head_lanes = (lane < HEAD_DIM, lane >= HEAD_DIM)
    nt_dims = (((1,), (1,)), ((), ()))
    for g in range(3):
        win, dil = WINDOWS[g], DILATIONS[g]
        q_ref, k_ref, v_ref = qkv[g]
        for r in range(dil):
            def unit(n, carry, g=g, win=win, dil=dil, r=r, q_ref=q_ref, k_ref=k_ref, v_ref=v_ref):
                cur = pl.multiple_of(n * win + r * BLK, BLK)
                prev = pl.multiple_of(jnp.maximum(n - 1, 0) * win + r * BLK, BLK)
                q = q_ref[0, pl.ds(cur, BLK), :] * QK_SCALE
                k = jnp.concatenate([k_ref[0, pl.ds(prev, BLK), :], k_ref[0, pl.ds(cur, BLK), :]], 0)
                v = jnp.concatenate([v_ref[0, pl.ds(prev, BLK), :], v_ref[0, pl.ds(cur, BLK), :]], 0)
                variant = jnp.minimum(n, 1)
                outs, lses = [], []
                for hh in range(2):
                    qm = jnp.where(head_lanes[hh], q, jnp.zeros_like(q))
                    s = lax.dot_general(qm, k, nt_dims, preferred_element_type=F32)
                    s = s + bias_ref[g, variant, hh]
                    m = jnp.max(s, axis=-1, keepdims=True)
                    p = jnp.exp(s - m)
                    l = jnp.sum(p, axis=-1, keepdims=True)
                    outs.append(jnp.dot(p.astype(BF16), v, preferred_element_type=F32) / l)
                    lses.append(m + jnp.log(l))
                o = jnp.where(head_lanes[0], outs[0], outs[1])
                lse = jnp.where(head_lanes[0], lses[0], lses[1])
                rows = pl.ds(pl.multiple_of(n * win, BLK) + r, BLK, stride=dil)
                o_scr[g, rows, :] = o
                l_scr[g, rows, :] = lse
                return carry

            lax.fori_loop(0, length // win, unit, 0, unroll=min(length // win, UNIT_UNROLL))

    merge_rows = 512
    for c in range(length // merge_rows):
        rows = slice(c * merge_rows, (c + 1) * merge_rows)
        l0, l1, l2 = l_scr[0, rows, :], l_scr[1, rows, :], l_scr[2, rows, :]
        m = jnp.maximum(jnp.maximum(l0, l1), l2)
        e0, e1, e2 = jnp.exp(l0 - m), jnp.exp(l1 - m), jnp.exp(l2 - m)
        num = e0 * o_scr[0, rows, :] + e1 * o_scr[1, rows, :] + e2 * o_scr[2, rows, :]
        o_ref[0, rows, :] = num / (e0 + e1 + e2)


def _prompt_bias():
    h = jnp.arange(ATT_HEADS, dtype=F32) + 1.0
    slopes = jnp.exp2(-ALIBI_BASE_EXP * h / ATT_HEADS)
    a = jnp.arange(BLK)[:, None]
    bk = jnp.arange(2 * BLK)[None, :]
    steps = a - bk + BLK
    ok = (steps >= 0) & (steps <= BLK)
    ok_first = ok & (bk >= BLK)
    tables = []
    for dil in DILATIONS:
        bias = -slopes[:, None, None] * (steps * dil).astype(F32)[None]
        tables.append(jnp.stack([jnp.where(ok_first[None], bias, -jnp.inf),
                                 jnp.where(ok[None], bias, -jnp.inf)], 0))
    return jnp.stack(tables, 0)


def _attn_core_prompt(qkv, bias):
    bsz, length, _ = qkv.shape
    n_hp = ATT_HEADS // 2
    col_tiles = ATT_HEADS * HEAD_DIM // LANE_TILE

    def col_spec(g, part):
        base = (g * 3 + part) * col_tiles
        return pl.BlockSpec((1, length, LANE_TILE), lambda b, hp: (b, 0, base + hp))

    in_specs = [col_spec(g, part) for g in range(3) for part in range(3)]
    in_specs.append(pl.BlockSpec((3, 2, 2, BLK, 2 * BLK), lambda b, hp: (0, 0, hp, 0, 0)))
    return pl.pallas_call(
        functools.partial(_attn_core_body, length=length),
        grid=(bsz, n_hp),
        in_specs=in_specs,
        out_specs=pl.BlockSpec((1, length, LANE_TILE), lambda b, hp: (b, 0, hp)),
        out_shape=jax.ShapeDtypeStruct((bsz, length, ATT_HEADS * HEAD_DIM), F32),
        scratch_shapes=[pltpu.VMEM((3, length, LANE_TILE), F32),
                        pltpu.VMEM((3, length, LANE_TILE), F32)],
        compiler_params=_cparams(("arbitrary", "arbitrary")),
        name="attn_core_prompt",
    )(*([qkv] * 9), bias)


SA_HEADS = 8
SA_ROWS = SA_HEADS * HEAD_DIM
DEC_SEQ = 8
NEG_BIG = -1e30

SA_PASSES = ([("g2", 2, 16), ("g1a", 1, 4), ("g1b", 1, 4)]
             + [("g0_%d" % t, 0, 1) for t in range(DEC_SEQ)])
SA_Q_TILE = {name: i for i, (name, _, _) in enumerate(SA_PASSES)}
SA_NEW = ("g2", "g1a", "g1b", "g0")
SA_K_TILE = {name: len(SA_PASSES) + i for i, name in enumerate(SA_NEW)}
SA_V_TILE = {name: len(SA_PASSES) + len(SA_NEW) + i for i, name in enumerate(SA_NEW)}
SA_TILES = len(SA_PASSES) + 2 * len(SA_NEW)


def _src_lane(part, g, t):
    return (part * 3 + g) * DEC_SEQ + t


def _sample_tables():
    lam = np.arange(LANE_TILE)
    none = np.full(LANE_TILE, -1)

    def tile(src):
        m = np.zeros((LANE_TILE, LANE_TILE), np.float32)
        ok = src >= 0
        m[src[ok], lam[ok]] = 1.0
        return m

    def new_src(part, name):
        if name == "g2":
            return np.where(lam < 8, _src_lane(part, 2, lam % 8), none)
        if name == "g1a":
            return np.where(lam < 4, _src_lane(part, 1, lam % 4), none)
        if name == "g1b":
            return np.where(lam < 4, _src_lane(part, 1, 4 + lam % 4),
                            np.where(lam < 8, _src_lane(part, 1, lam % 4), none))
        return np.where(lam < 8, _src_lane(part, 0, lam % 8), none)

    tiles = []
    for name, g, _ in SA_PASSES:
        if name == "g2":
            tiles.append(tile(np.where(lam % 16 < 8, _src_lane(0, 2, lam % 8), none)))
        elif name == "g1a":
            tiles.append(tile(_src_lane(0, 1, lam % 4)))
        elif name == "g1b":
            tiles.append(tile(_src_lane(0, 1, 4 + lam % 4)))
        else:
            tiles.append(tile(np.full(LANE_TILE, _src_lane(0, 0, int(name[3:])))))
    for part in (1, 2):
        for name in SA_NEW:
            tiles.append(tile(new_src(part, name)))
    place = np.concatenate(tiles, axis=1)

    h = jnp.arange(ATT_HEADS, dtype=F32) + 1.0
    slopes = jnp.exp2(-ALIBI_BASE_EXP * h / ATT_HEADS)[:, None]

    def bias(dist, ok):
        d = jnp.asarray(np.where(ok, dist, 0), F32)[None, :]
        return jnp.where(jnp.asarray(ok)[None, :], -slopes * d, -jnp.inf)

    def table(win, cache_dist, cache_ok, new_dist, new_ok):
        rho = np.arange(win)
        return jnp.concatenate([bias(cache_dist(rho), cache_ok(rho)),
                                bias(new_dist(lam), new_ok(lam))], axis=1)

    zero = lambda l: 0 * l
    b2 = table(2048, lambda r: 2048 - 16 * (r // 16), lambda r: r % 16 < 8, zero, lambda l: l < 8)
    b1a = table(512, lambda r: 512 - 4 * (r // 4), lambda r: r >= 0, zero, lambda l: l < 4)
    b1b = table(512, lambda r: 516 - 4 * (r // 4), lambda r: r >= 4,
                lambda l: np.where(l < 4, 0, 4), lambda l: l < 8)
    b0 = [table(128, lambda r, t=t: 128 + t - r, lambda r, t=t: r >= t,
                lambda l, t=t: t - l, lambda l, t=t: l <= t) for t in range(DEC_SEQ)]
    expand = np.zeros((LANE_TILE, SA_ROWS), np.float32)
    for hh in range(SA_HEADS):
        expand[hh, hh * HEAD_DIM:(hh + 1) * HEAD_DIM] = 1.0
    return (jnp.asarray(place, BF16), b2, jnp.stack([b1a, b1b], 0), jnp.stack(b0, 0),
            jnp.asarray(expand, BF16))


def _class_reduce(x, op, period):
    if period == 1:
        red = jnp.max if op is jnp.maximum else jnp.sum
        return jnp.broadcast_to(red(x, axis=1, keepdims=True), x.shape)
    shift = period
    while shift < LANE_TILE:
        x = op(x, pltpu.roll(x, shift, axis=1))
        shift *= 2
    return x


def _split_dot(a, b_bf):
    hi = a.astype(BF16)
    lo = (a - hi.astype(F32)).astype(BF16)
    return (jnp.dot(hi, b_bf, preferred_element_type=F32)
            + jnp.dot(lo, b_bf, preferred_element_type=F32))


def _attn_sample_body(xt_ref, c0_ref, c1_ref, c2_ref, place_ref, b2_ref, b1_ref, b0_ref, exp_ref,
                      o_ref, pl_scr, s_scr, r_scr, st_scr):
    lt = LANE_TILE
    xt = xt_ref[0].astype(BF16)
    for c in range(SA_TILES):
        pl_scr[c] = jnp.dot(xt, place_ref[:, c * lt:(c + 1) * lt], preferred_element_type=F32)
    st_scr[...] = jnp.zeros_like(st_scr)
    lane = lax.broadcasted_iota(jnp.int32, (1, lt), 1)
    caches = (c0_ref, c1_ref, c2_ref)
    biases = {"g2": lambda: b2_ref[...], "g1a": lambda: b1_ref[0], "g1b": lambda: b1_ref[1]}
    acc_slot = {"g2": 0, "g1a": 1, "g1b": 2}

    def head_rows(h):
        return slice(h * HEAD_DIM, (h + 1) * HEAD_DIM)

    for name, g, period in SA_PASSES:
        win = WINDOWS[g]
        n_ct = win // lt
        cache = caches[g]
        new_name = name if g else "g0"
        t0 = int(name[3:]) if g == 0 else None
        q_tile, k_tile, v_tile = SA_Q_TILE[name], SA_K_TILE[new_name], SA_V_TILE[new_name]
        for h in range(SA_HEADS):
            q = pl_scr[q_tile, head_rows(h), :]
            for j in range(n_ct):
                kt = cache[0, 0, 0, h, :, j * lt:(j + 1) * lt]
                s_scr[h:h + 1, j * lt:(j + 1) * lt] = jnp.sum(q * kt, axis=0, keepdims=True)
            kn = pl_scr[k_tile, head_rows(h), :]
            s_scr[h:h + 1, win:win + lt] = jnp.sum(q * kn, axis=0, keepdims=True)

        bias = b0_ref[t0] if g == 0 else biases[name]()
        tiles = [s_scr[:, j * lt:(j + 1) * lt] * QK_SCALE + bias[:, j * lt:(j + 1) * lt]
                 for j in range(n_ct + 1)]
        m = _class_reduce(functools.reduce(jnp.maximum, tiles), jnp.maximum, period)
        m = jnp.maximum(m, NEG_BIG)
        ps = [jnp.exp(tl - m) for tl in tiles]
        l = _class_reduce(functools.reduce(lambda a, b: a + b, ps), lambda a, b: a + b, period)
        l = jnp.maximum(l, -1.0 / NEG_BIG)
        inv = 1.0 / l
        for j in range(n_ct + 1):
            s_scr[:, j * lt:(j + 1) * lt] = ps[j] * inv
        lse = m + jnp.log(l)
        if g == 0:
            st_scr[3, 0:SA_HEADS, :] = jnp.where(lane == t0, lse, st_scr[3, 0:SA_HEADS, :])
        else:
            st_scr[acc_slot[name], 0:SA_HEADS, :] = lse

        for h in range(SA_HEADS):
            acc = s_scr[h:h + 1, win:win + lt] * pl_scr[v_tile, head_rows(h), :]
            for j in range(n_ct):
                acc = acc + (s_scr[h:h + 1, j * lt:(j + 1) * lt]
                             * cache[0, 0, 1, h, :, j * lt:(j + 1) * lt])
            if g == 0:
                tot = jnp.sum(acc, axis=1, keepdims=True)
                prev = r_scr[3, head_rows(h), :] if t0 else jnp.zeros((HEAD_DIM, lt), F32)
                r_scr[3, head_rows(h), :] = jnp.where(lane == t0, tot, prev)
            else:
                r_scr[acc_slot[name], head_rows(h), :] = acc

    def by_query(slot):
        rt = r_scr[slot].T
        return functools.reduce(lambda a, b: a + b,
                                [rt[8 * j:8 * (j + 1), :] for j in range(lt // 8)])

    def stats_by_query(slot):
        return st_scr[slot].T[0:DEC_SEQ, :]

    row = lax.broadcasted_iota(jnp.int32, (DEC_SEQ, 1), 0)
    low = row < 4
    o2 = by_query(0)
    oa, ob = by_query(1), by_query(2)
    o1 = jnp.where(low, oa + pltpu.roll(oa, 4, axis=0), ob + pltpu.roll(ob, 4, axis=0))
    o0 = by_query(3)
    l2 = stats_by_query(0)
    l1 = jnp.where(low, stats_by_query(1), pltpu.roll(stats_by_query(2), 4, axis=0))
    l0 = stats_by_query(3)
    mx = jnp.maximum(jnp.maximum(l0, l1), l2)
    e0, e1, e2 = jnp.exp(l0 - mx), jnp.exp(l1 - mx), jnp.exp(l2 - mx)
    den = e0 + e1 + e2
    w = jnp.concatenate([e0 / den, e1 / den, e2 / den], axis=0)
    w = _split_dot(w, exp_ref[...])
    o_ref[0] = (w[0:DEC_SEQ] * o0 + w[DEC_SEQ:2 * DEC_SEQ] * o1 + w[2 * DEC_SEQ:] * o2)


def _attn_core_sample(proj, caches_t, tables, layer):
    dbsz, tlen, _ = proj.shape
    width = ATT_HEADS * HEAD_DIM
    n_hb = ATT_HEADS // SA_HEADS
    xt = proj.reshape(dbsz, tlen, 3, 3, width).transpose(0, 4, 3, 2, 1).reshape(dbsz, width, 9 * tlen)
    xt = jnp.pad(xt, ((0, 0), (0, 0), (0, LANE_TILE - 9 * tlen)))
    place, b2, b1, b0, expand = tables

    def cache_spec(win):
        return pl.BlockSpec((1, 1, 2, SA_HEADS, HEAD_DIM, win),
                            lambda b, hb: (layer, b, 0, hb, 0, 0))

    max_ext = WINDOWS[2] + LANE_TILE
    return pl.pallas_call(
        _attn_sample_body,
        grid=(dbsz, n_hb),
        in_specs=[pl.BlockSpec((1, SA_ROWS, LANE_TILE), lambda b, hb: (b, hb, 0)),
                  cache_spec(WINDOWS[0]), cache_spec(WINDOWS[1]), cache_spec(WINDOWS[2]),
                  pl.BlockSpec(place.shape, lambda b, hb: (0, 0)),
                  pl.BlockSpec((SA_HEADS, b2.shape[1]), lambda b, hb: (hb, 0)),
                  pl.BlockSpec((2, SA_HEADS, b1.shape[2]), lambda b, hb: (0, hb, 0)),
                  pl.BlockSpec((DEC_SEQ, SA_HEADS, b0.shape[2]), lambda b, hb: (0, hb, 0)),
                  pl.BlockSpec(expand.shape, lambda b, hb: (0, 0))],
        out_specs=pl.BlockSpec((1, tlen, SA_ROWS), lambda b, hb: (b, 0, hb)),
        out_shape=jax.ShapeDtypeStruct((dbsz, tlen, width), F32),
        scratch_shapes=[pltpu.VMEM((SA_TILES, SA_ROWS, LANE_TILE), F32),
                        pltpu.VMEM((SA_HEADS, max_ext), F32),
                        pltpu.VMEM((4, SA_ROWS, LANE_TILE), F32),
                        pltpu.VMEM((4, LANE_TILE, LANE_TILE), F32)],
        compiler_params=_cparams(("arbitrary", "arbitrary")),
        name="attn_core_sample",
    )(xt, caches_t[0], caches_t[1], caches_t[2], place, b2, b1, b0, expand)


def kernel(x_prompt, x_sample, c_prompt, c_sample, state_ssm_re, state_ssm_im, cache_kv_w128, cache_kv_w512, cache_kv_w2048, w_ada, b_ada, ln_g, ln_b, ssm_w_in, ssm_a_re, ssm_a_im, ssm_log_dt, ssm_b_re, ssm_b_im, ssm_c_re, ssm_c_im, ssm_d, ssm_w_glu, ssm_b_glu, ssm_w_out, attn_w_in, attn_w_out):
    bsz, seq, d = x_prompt.shape
    dbsz, dec_seq, _ = x_sample.shape
    n_dec = dbsz * dec_seq
    att_width = ATT_HEADS * HEAD_DIM

    pad = (-(bsz + dbsz)) % 8
    c_all = jnp.concatenate([c_prompt, c_sample, jnp.zeros((pad, d), c_prompt.dtype)], 0)
    mod = _ada(c_all, w_ada, b_ada)
    bias_p = _prompt_bias()
    sample_tables = _sample_tables()
    caches_t = [jnp.transpose(c, (0, 1, 3, 4, 5, 2))
                for c in (cache_kv_w128, cache_kv_w512, cache_kv_w2048)]

    xp = x_prompt
    xs = x_sample.reshape(1, n_dec, d)
    ssm_p_re, ssm_p_im, ssm_s_re, ssm_s_im = [], [], [], []
    kv_p = [[], [], []]
    kv_s = [[], [], []]
    for i in range(DEPTH):
        j = i // 2
        mod_p = mod[i, :bsz]
        mod_s = jnp.repeat(mod[i, bsz:bsz + dbsz], dec_seq, axis=0)
        sh_p, sc_p, gt_p = [mod_p[:, None, k * d:(k + 1) * d] for k in range(3)]
        sh_s, sc_s, gt_s = [mod_s[None, :, k * d:(k + 1) * d] for k in range(3)]
        if i % 2 == 0:
            prep = _ssm_prepare(ssm_a_re[j], ssm_a_im[j], ssm_log_dt[j], ssm_b_re[j], ssm_b_im[j],
                                ssm_c_re[j], ssm_c_im[j], ssm_d[j], dec_seq)
            w_in = ssm_w_in[j].astype(BF16)
            w_s, w_z = w_in[:, :d], w_in[:, d:]
            w_glu = ssm_w_glu[j].astype(BF16)
            w_out = ssm_w_out[j].astype(BF16)

            s_p = _modmm(xp, sc_p, sh_p, w_s, 1024, d)
            y_p, h_p = _ssm_core_prompt(s_p, prep, 512)
            hr, hi = _state_from_kt(h_p)
            ssm_p_re.append(hr)
            ssm_p_im.append(hi)
            xp = _tail(xp, sc_p, sh_p, gt_p, y_p, w_z, w_out, ln_g[i], ln_b[i], 512,
                       wglu_bf=w_glu, bglu=ssm_b_glu[j])

            s_s = _modmm(xs, sc_s, sh_s, w_s, n_dec, d)
            h0 = _state_to_kt(state_ssm_re[j], state_ssm_im[j])
            y_s, h_s = _ssm_core_sample(s_s[0], h0, prep, dec_seq)
            hr, hi = _state_from_kt(h_s.transpose(1, 0, 2))
            ssm_s_re.append(hr)
            ssm_s_im.append(hi)
            xs = _tail(xs, sc_s, sh_s, gt_s, y_s[None], w_z, w_out, ln_g[i], ln_b[i], 512,
                       wglu_bf=w_glu, bglu=ssm_b_glu[j])
        else:
            w_in = attn_w_in[j].astype(BF16)
            w_qkv, w_z = w_in[:, :ATT_QKV], w_in[:, ATT_QKV:]
            w_out = attn_w_out[j].astype(BF16)

            qkv, kv0, kv1, kv2 = _attn_inproj_prompt(xp, sc_p, sh_p, w_qkv)
            for g, kv in enumerate((kv0, kv1, kv2)):
                kv_p[g].append(kv.reshape(bsz, WINDOWS[g], 2, ATT_HEADS, HEAD_DIM))
            o_p = _attn_core_prompt(qkv, bias_p)
            xp = _tail(xp, sc_p, sh_p, gt_p, o_p, w_z, w_out, ln_g[i], ln_b[i], 512)

            proj_s = _modmm(xs, sc_s, sh_s, w_qkv, n_dec, 1024)[0].reshape(dbsz, dec_seq, ATT_QKV)
            new_s = proj_s.reshape(dbsz, dec_seq, 3, 3, ATT_HEADS, HEAD_DIM)[:, :, :, 1:]
            for g in range(3):
                kv_s[g].append(new_s[:, :, g])
            o_s = _attn_core_sample(proj_s, caches_t, sample_tables, j)
            xs = _tail(xs, sc_s, sh_s, gt_s, o_s.reshape(1, n_dec, att_width), w_z, w_out,
                       ln_g[i], ln_b[i], 512)

    return (xp, xs.reshape(dbsz, dec_seq, d),
            jnp.stack(ssm_p_re, 0), jnp.stack(ssm_p_im, 0),
            jnp.stack(kv_p[0], 0), jnp.stack(kv_p[1], 0), jnp.stack(kv_p[2], 0),
            jnp.stack(ssm_s_re, 0), jnp.stack(ssm_s_im, 0),
            jnp.stack(kv_s[0], 0), jnp.stack(kv_s[1], 0), jnp.stack(kv_s[2], 0))
```

```python
import functools

import numpy as np
import jax
import jax.numpy as jnp
from jax import lax
from jax.experimental import pallas as pl
from jax.experimental.pallas import tpu as pltpu

F32 = jnp.float32
BF16 = jnp.bfloat16

D_MODEL = 1024
DEPTH = 4
SSM_GROUPS = 64
SSM_GROUP = 16
SSM_STATE = 64
ATT_HEADS = 16
HEAD_DIM = 64
WINDOWS = (128, 512, 2048)
DILATIONS = (1, 4, 16)
ATT_QKV = 3 * 3 * ATT_HEADS * HEAD_DIM
ALIBI_BASE_EXP = 8.0
DEEPNORM_ALPHA = (2 * DEPTH) ** 0.25
LN_EPS = 1e-5
QK_SCALE = HEAD_DIM ** -0.5

LANE_TILE = 128
BLK = 128
SSM_CHUNK = 128
SSM_CENTER = SSM_CHUNK // 2
SSM_KT = D_MODEL // LANE_TILE
GROUPS_PER_KT = LANE_TILE // SSM_GROUP
KT_STATE = GROUPS_PER_KT * SSM_STATE
VMEM_LIMIT = 56 * 1024 * 1024


def _cparams(sem):
    return pltpu.CompilerParams(dimension_semantics=sem, vmem_limit_bytes=VMEM_LIMIT)


def _ada_body(c_ref, w_ref, b_ref, o_ref):
    c = c_ref[...]
    s = c * jax.nn.sigmoid(c)
    o_ref[0] = jnp.dot(s.astype(BF16), w_ref[0].astype(BF16), preferred_element_type=F32) + b_ref[0]


def _ada(c_all, w_ada, b_ada):
    rows = c_all.shape[0]
    d = D_MODEL
    return pl.pallas_call(
        _ada_body,
        grid=(DEPTH, 3),
        in_specs=[pl.BlockSpec((rows, d), lambda i, n: (0, 0)),
                  pl.BlockSpec((1, d, d), lambda i, n: (i, 0, n)),
                  pl.BlockSpec((1, 1, d), lambda i, n: (i, 0, n))],
        out_specs=pl.BlockSpec((1, rows, d), lambda i, n: (i, 0, n)),
        out_shape=jax.ShapeDtypeStruct((DEPTH, rows, 3 * d), F32),
        compiler_params=_cparams(("arbitrary", "arbitrary")),
        name="ada",
    )(c_all, w_ada, b_ada.reshape(DEPTH, 1, 3 * d))


def _modmm_body(x_ref, sc_ref, sh_ref, w_ref, o_ref, u_scr):
    @pl.when(pl.program_id(2) == 0)
    def _():
        u_scr[...] = (x_ref[0] * (1.0 + sc_ref[0]) + sh_ref[0]).astype(BF16)

    o_ref[0] = jnp.dot(u_scr[...], w_ref[...], preferred_element_type=F32)


def _mod_index(per_token):
    if per_token:
        return lambda b, m, n: (b, m, 0)
    return lambda b, m, n: (b, 0, 0)


def _modmm(x, sc, sh, w_bf, tm, tn):
    bsz, length, d = x.shape
    n_out = w_bf.shape[1]
    per_token = sc.shape[1] != 1
    mod_rows = tm if per_token else 1
    return pl.pallas_call(
        _modmm_body,
        grid=(bsz, length // tm, n_out // tn),
        in_specs=[pl.BlockSpec((1, tm, d), lambda b, m, n: (b, m, 0)),
                  pl.BlockSpec((1, mod_rows, d), _mod_index(per_token)),
                  pl.BlockSpec((1, mod_rows, d), _mod_index(per_token)),
                  pl.BlockSpec((d, tn), lambda b, m, n: (0, n))],
        out_specs=pl.BlockSpec((1, tm, tn), lambda b, m, n: (b, m, n)),
        out_shape=jax.ShapeDtypeStruct((bsz, length, n_out), F32),
        scratch_shapes=[pltpu.VMEM((tm, d), BF16)],
        compiler_params=_cparams(("arbitrary", "arbitrary", "arbitrary")),
        name="mod_proj",
    )(x, sc, sh, w_bf)


def _tail_body(*refs, glu):
    if glu:
        (x_ref, sc_ref, sh_ref, gt_ref, y_ref, wz_ref, wglu_ref, bglu_ref, wout_ref,
         lng_ref, lnb_ref, o_ref) = refs
    else:
        (x_ref, sc_ref, sh_ref, gt_ref, y_ref, wz_ref, wout_ref, lng_ref, lnb_ref, o_ref) = refs
    x = x_ref[0]
    u = (x * (1.0 + sc_ref[0]) + sh_ref[0]).astype(BF16)
    z = jnp.dot(u, wz_ref[...], preferred_element_type=F32)
    y = y_ref[0]
    if glu:
        y = y * jax.nn.sigmoid(
            jnp.dot(y.astype(BF16), wglu_ref[...], preferred_element_type=F32) + bglu_ref[...])
    y = y * (z * jax.nn.sigmoid(z))
    out = jnp.dot(y.astype(BF16), wout_ref[...], preferred_element_type=F32)
    r = DEEPNORM_ALPHA * x + gt_ref[0] * out
    mu = jnp.mean(r, axis=-1, keepdims=True)
    rc = r - mu
    var = jnp.mean(rc * rc, axis=-1, keepdims=True)
    o_ref[0] = rc * lax.rsqrt(var + LN_EPS) * lng_ref[...] + lnb_ref[...]


def _tail(x, sc, sh, gt, y, wz_bf, wout_bf, ln_g, ln_b, tm, wglu_bf=None, bglu=None):
    bsz, length, d = x.shape
    per_token = sc.shape[1] != 1
    mod_rows = tm if per_token else 1
    if per_token:
        mod_idx = lambda b, m: (b, m, 0)
    else:
        mod_idx = lambda b, m: (b, 0, 0)
    row_spec = pl.BlockSpec((1, tm, d), lambda b, m: (b, m, 0))
    mod_spec = pl.BlockSpec((1, mod_rows, d), mod_idx)
    w_spec = pl.BlockSpec((d, d), lambda b, m: (0, 0))
    vec_spec = pl.BlockSpec((1, d), lambda b, m: (0, 0))
    glu = wglu_bf is not None
    operands = [x, sc, sh, gt, y, wz_bf]
    in_specs = [row_spec, mod_spec, mod_spec, mod_spec, row_spec, w_spec]
    if glu:
        operands += [wglu_bf, bglu.reshape(1, d)]
        in_specs += [w_spec, vec_spec]
    operands += [wout_bf, ln_g.reshape(1, d), ln_b.reshape(1, d)]
    in_specs += [w_spec, vec_spec, vec_spec]
    return pl.pallas_call(
        functools.partial(_tail_body, glu=glu),
        grid=(bsz, length // tm),
        in_specs=in_specs,
        out_specs=row_spec,
        out_shape=jax.ShapeDtypeStruct((bsz, length, d), F32),
        compiler_params=_cparams(("arbitrary", "arbitrary")),
        name="tail_glu" if glu else "tail",
    )(*operands)


def _complex_mul(ar, ai, br, bi):
    return ar * br - ai * bi, ar * bi + ai * br


def _prefix_matmul(l_mat, g):
    hi = g.astype(BF16)
    lo = (g - hi.astype(F32)).astype(BF16)
    return (jnp.dot(l_mat, hi, preferred_element_type=F32)
            + jnp.dot(l_mat, lo, preferred_element_type=F32))


def _ssm_scaled_inputs(s_c, bdb, tn):
    ks = KT_STATE
    bu = jnp.dot(s_c.astype(BF16), bdb, preferred_element_type=F32)
    return _complex_mul(tn[:, :ks], tn[:, ks:], bu[:, :ks], bu[:, ks:])


def _ssm_outputs(s_c, p_re, p_im, tp, bdc, dsk):
    ks = KT_STATE
    h_re, h_im = _complex_mul(tp[:, :ks], tp[:, ks:], p_re, p_im)
    y = (jnp.dot(h_re.astype(BF16), bdc[:ks], preferred_element_type=F32)
         + jnp.dot(h_im.astype(BF16), bdc[ks:], preferred_element_type=F32))
    return jax.nn.gelu(y + dsk * s_c), h_re, h_im


def _ssm_prompt_body(s_ref, bdb_ref, bdc_ref, tn_ref, tp_ref, l_ref, cm_ref, d_ref,
                     y_ref, h_ref, carry_scr, *, n_chunks):
    ks = KT_STATE
    t = SSM_CHUNK

    @pl.when(pl.program_id(2) == 0)
    def _():
        carry_scr[...] = jnp.zeros_like(carry_scr)

    l_mat = l_ref[...]
    tn = tn_ref[0]
    tp = tp_ref[0]
    cm = cm_ref[0]
    bdb = bdb_ref[0]
    bdc = bdc_ref[0]
    dsk = d_ref[...]
    sums = []
    for ci in range(n_chunks):
        s_c = s_ref[0, ci * t:(ci + 1) * t, :]
        g_re, g_im = _ssm_scaled_inputs(s_c, bdb, tn)
        sums.append((_prefix_matmul(l_mat, g_re), _prefix_matmul(l_mat, g_im)))
    hp = carry_scr[0:1, :]
    carried = []
    for p_re, p_im in sums:
        c_re, c_im = _complex_mul(cm[:, :ks], cm[:, ks:], hp[:, :ks], hp[:, ks:])
        carried.append((c_re, c_im))
        e_re, e_im = _complex_mul(tp[t - 1:t, :ks], tp[t - 1:t, ks:],
                                  p_re[t - 1:t] + c_re, p_im[t - 1:t] + c_im)
        hp = jnp.concatenate([e_re, e_im], axis=1)
    for ci in range(n_chunks):
        s_c = s_ref[0, ci * t:(ci + 1) * t, :]
        (p_re, p_im), (c_re, c_im) = sums[ci], carried[ci]
        y, _, _ = _ssm_outputs(s_c, p_re + c_re, p_im + c_im, tp, bdc, dsk)
        y_ref[0, ci * t:(ci + 1) * t, :] = y
    carry_scr[0:1, :] = hp
    h_ref[0, 0] = jnp.broadcast_to(hp, (8, 2 * ks))


def _ssm_core_prompt(s, prep, tt):
    bsz, length, d = s.shape
    t = SSM_CHUNK
    w2 = 2 * KT_STATE
    y, h = pl.pallas_call(
        functools.partial(_ssm_prompt_body, n_chunks=tt // t),
        grid=(bsz, SSM_KT, length // tt),
        in_specs=[pl.BlockSpec((1, tt, LANE_TILE), lambda b, k, c: (b, c, k)),
                  pl.BlockSpec((1, LANE_TILE, w2), lambda b, k, c: (k, 0, 0)),
                  pl.BlockSpec((1, w2, LANE_TILE), lambda b, k, c: (k, 0, 0)),
                  pl.BlockSpec((1, t, w2), lambda b, k, c: (k, 0, 0)),
                  pl.BlockSpec((1, t, w2), lambda b, k, c: (k, 0, 0)),
                  pl.BlockSpec((t, t), lambda b, k, c: (0, 0)),
                  pl.BlockSpec((1, 1, w2), lambda b, k, c: (k, 0, 0)),
                  pl.BlockSpec((1, LANE_TILE), lambda b, k, c: (0, k))],
        out_specs=[pl.BlockSpec((1, tt, LANE_TILE), lambda b, k, c: (b, c, k)),
                   pl.BlockSpec((1, 1, 8, w2), lambda b, k, c: (b, k, 0, 0))],
        out_shape=[jax.ShapeDtypeStruct((bsz, length, d), F32),
                   jax.ShapeDtypeStruct((bsz, SSM_KT, 8, w2), F32)],
        scratch_shapes=[pltpu.VMEM((8, w2), F32)],
        compiler_params=_cparams(("arbitrary", "arbitrary", "arbitrary")),
        name="ssm_core_prompt",
    )(s, prep["bdb"], prep["bdc"], prep["tn_p"], prep["tp_p"], prep["l_p"], prep["cm_p"],
      prep["dsk"])
    return y, h[:, :, 0, :]


def _lane_tiles_store(scr, v):
    for c in range(v.shape[1] // LANE_TILE):
        scr[c] = v[:, c * LANE_TILE:(c + 1) * LANE_TILE]


def _lane_tiles_load(scr, rows):
    return jnp.concatenate([scr[c, rows, :] for c in range(scr.shape[0])], axis=1)


def _ssm_sample_body(s_ref, h0_ref, bdb_ref, bdc_ref, tn_ref, tp_ref, l_ref, cm_ref, d_ref,
                     y_ref, h_ref, gre_scr, gim_scr, *, dec_seq):
    ks = KT_STATE
    nb = SSM_CHUNK // dec_seq
    cm = cm_ref[0]
    h0 = h0_ref[0]
    s_c = s_ref[...]
    g_re, g_im = _ssm_scaled_inputs(s_c, bdb_ref[0], tn_ref[0])
    c_re, c_im = _complex_mul(cm[:, :ks], cm[:, ks:], h0[:, :ks], h0[:, ks:])
    first = pl.ds(0, nb, stride=dec_seq)
    _lane_tiles_store(gre_scr, g_re)
    _lane_tiles_store(gim_scr, g_im)
    for c in range(ks // LANE_TILE):
        cols = slice(c * LANE_TILE, (c + 1) * LANE_TILE)
        gre_scr[c, first, :] = gre_scr[c, first, :] + c_re[:, cols]
        gim_scr[c, first, :] = gim_scr[c, first, :] + c_im[:, cols]
    l_mat = l_ref[...]
    p_re = _prefix_matmul(l_mat, _lane_tiles_load(gre_scr, slice(None)))
    p_im = _prefix_matmul(l_mat, _lane_tiles_load(gim_scr, slice(None)))
    y, h_re, h_im = _ssm_outputs(s_c, p_re, p_im, tp_ref[0], bdc_ref[0], d_ref[...])
    y_ref[...] = y
    _lane_tiles_store(gre_scr, h_re)
    _lane_tiles_store(gim_scr, h_im)
    last = pl.ds(dec_seq - 1, nb, stride=dec_seq)
    h_ref[0, :, :ks] = _lane_tiles_load(gre_scr, last)
    h_ref[0, :, ks:] = _lane_tiles_load(gim_scr, last)


def _ssm_core_sample(s, h0, prep, dec_seq):
    rows, d = s.shape
    t = SSM_CHUNK
    nb = t // dec_seq
    w2 = 2 * KT_STATE
    return pl.pallas_call(
        functools.partial(_ssm_sample_body, dec_seq=dec_seq),
        grid=(SSM_KT, rows // t),
        in_specs=[pl.BlockSpec((t, LANE_TILE), lambda k, c: (c, k)),
                  pl.BlockSpec((1, nb, w2), lambda k, c: (k, c, 0)),
                  pl.BlockSpec((1, LANE_TILE, w2), lambda k, c: (k, 0, 0)),
                  pl.BlockSpec((1, w2, LANE_TILE), lambda k, c: (k, 0, 0)),
                  pl.BlockSpec((1, t, w2), lambda k, c: (k, 0, 0)),
                  pl.BlockSpec((1, t, w2), lambda k, c: (k, 0, 0)),
                  pl.BlockSpec((t, t), lambda k, c: (0, 0)),
                  pl.BlockSpec((1, 1, w2), lambda k, c: (k, 0, 0)),
                  pl.BlockSpec((1, LANE_TILE), lambda k, c: (0, k))],
        out_specs=[pl.BlockSpec((t, LANE_TILE), lambda k, c: (c, k)),
                   pl.BlockSpec((1, nb, w2), lambda k, c: (k, c, 0))],
        out_shape=[jax.ShapeDtypeStruct((rows, d), F32),
                   jax.ShapeDtypeStruct((SSM_KT, rows // dec_seq, w2), F32)],
        scratch_shapes=[pltpu.VMEM((KT_STATE // LANE_TILE, t, LANE_TILE), F32)] * 2,
        compiler_params=_cparams(("arbitrary", "arbitrary")),
        name="ssm_core_sample",
    )(s, h0, prep["bdb"], prep["bdc"], prep["tn_s"], prep["tp_s"], prep["l_s"], prep["cm_s"],
      prep["dsk"])


def _to_kt_lanes(re, im):
    rows = re.shape[0]

    def tile(v):
        return v.reshape(rows, SSM_KT, KT_STATE).transpose(1, 0, 2)

    return jnp.concatenate([tile(re), tile(im)], axis=-1)


def _ssm_prepare(a_re, a_im, log_dt, b_re, b_im, c_re, c_im, d_skip, dec_seq):
    dt = jnp.exp(log_dt)[:, None]
    lr, li = a_re, a_im
    mag = jnp.exp(lr * dt)
    abar_re = mag * jnp.cos(li * dt)
    abar_im = mag * jnp.sin(li * dt)
    den = lr * lr + li * li
    xr = abar_re - 1.0
    g_re = (xr * lr + abar_im * li) / den
    g_im = (abar_im * lr - xr * li) / den
    bbar_re = g_re[..., None] * b_re - g_im[..., None] * b_im
    bbar_im = g_re[..., None] * b_im + g_im[..., None] * b_re

    eye = jnp.eye(GROUPS_PER_KT, dtype=a_re.dtype)

    def in_map(bb):
        bb = bb.reshape(SSM_KT, GROUPS_PER_KT, SSM_STATE, SSM_GROUP)
        m = jnp.einsum("ab,kbpc->kacbp", eye, bb)
        return m.reshape(SSM_KT, LANE_TILE, KT_STATE)

    def out_map(cc):
        cc = cc.reshape(SSM_KT, GROUPS_PER_KT, SSM_GROUP, SSM_STATE)
        m = jnp.einsum("ab,kbcp->kapbc", eye, cc)
        return m.reshape(SSM_KT, KT_STATE, LANE_TILE)

    bdb = jnp.concatenate([in_map(bbar_re), in_map(bbar_im)], axis=-1).astype(BF16)
    bdc = jnp.concatenate([out_map(c_re), out_map(-c_im)], axis=1).astype(BF16)

    t = SSM_CHUNK
    c = SSM_CENTER
    log_mag = (lr * dt).reshape(SSM_KT, 1, KT_STATE)
    phase = (li * dt).reshape(SSM_KT, 1, KT_STATE)

    def power_rows(exps):
        k = jnp.asarray(exps, dtype=a_re.dtype).reshape(1, len(exps), 1)
        mag_k = jnp.exp(k * log_mag)
        return jnp.concatenate([mag_k * jnp.cos(k * phase), mag_k * jnp.sin(k * phase)], axis=-1)

    tn_p = power_rows([c - j for j in range(t)])
    tp_p = power_rows([i - c for i in range(t)])
    cm_p = power_rows([c + 1])
    tn_s = power_rows([-(r % dec_seq) for r in range(t)])
    tp_s = power_rows([r % dec_seq for r in range(t)])
    cm_s = power_rows([1])

    idx = jnp.arange(t)
    l_p = (idx[:, None] >= idx[None, :]).astype(BF16)
    l_s = ((idx[:, None] >= idx[None, :])
           & (idx[:, None] // dec_seq == idx[None, :] // dec_seq)).astype(BF16)
    return dict(bdb=bdb, bdc=bdc, tn_p=tn_p, tp_p=tp_p, cm_p=cm_p, tn_s=tn_s, tp_s=tp_s,
                cm_s=cm_s, l_p=l_p, l_s=l_s, dsk=d_skip.reshape(1, D_MODEL))


def _state_to_kt(h_re, h_im):
    return _to_kt_lanes(h_re, h_im)


def _state_from_kt(h):
    bsz = h.shape[0]
    re = h[:, :, :KT_STATE].reshape(bsz, SSM_GROUPS, SSM_STATE)
    im = h[:, :, KT_STATE:].reshape(bsz, SSM_GROUPS, SSM_STATE)
    return re, im


ATT_MT = 2048
ATT_TN = 512
QKV_TILES = ATT_QKV // ATT_TN
TILES_PER_GROUP = QKV_TILES // 3
KV_FIRST_TILE = TILES_PER_GROUP // 3
KV_TILES = 2 * TILES_PER_GROUP // 3
UNIT_UNROLL = 4


def _attn_inproj_body(x_ref, sc_ref, sh_ref, w_ref, qkv_ref, kv0_ref, kv1_ref, kv2_ref,
                      u_scr, acc_scr, *, last_mt):
    mt = pl.program_id(1)
    nt = pl.program_id(2)

    @pl.when(nt == 0)
    def _():
        u_scr[...] = (x_ref[0] * (1.0 + sc_ref[0]) + sh_ref[0]).astype(BF16)

    _lane_tiles_store(acc_scr, jnp.dot(u_scr[...], w_ref[...], preferred_element_type=F32))
    n_ct = ATT_TN // LANE_TILE
    group = nt // TILES_PER_GROUP
    kv_refs = (kv0_ref, kv1_ref, kv2_ref)
    for g in range(3):
        win, dil = WINDOWS[g], DILATIONS[g]

        @pl.when(group == g)
        def _(win=win, dil=dil):
            for c in range(n_ct):
                cols = slice(c * LANE_TILE, (c + 1) * LANE_TILE)
                if dil == 1:
                    qkv_ref[0, :, cols] = acc_scr[c].astype(BF16)
                    continue
                for n in range(ATT_MT // win):
                    for r in range(dil):
                        src = acc_scr[c, pl.ds(n * win + r, BLK, stride=dil), :]
                        dst = n * win + r * BLK
                        qkv_ref[0, dst:dst + BLK, cols] = src.astype(BF16)

        first = g * TILES_PER_GROUP + KV_FIRST_TILE

        @pl.when((mt == last_mt) & (nt >= first) & (nt < first + KV_TILES))
        def _(win=win, kv_ref=kv_refs[g]):
            for c in range(n_ct):
                kv_ref[0, :, c * LANE_TILE:(c + 1) * LANE_TILE] = acc_scr[c, ATT_MT - win:, :]


def _attn_inproj_prompt(x, sc, sh, w_bf):
    bsz, length, d = x.shape
    n_mt = length // ATT_MT
    last_mt = n_mt - 1
    kv_width = KV_TILES * ATT_TN

    def kv_index(g):
        first = g * TILES_PER_GROUP + KV_FIRST_TILE

        def index(b, m, n):
            col = jnp.clip(n - first, 0, KV_TILES - 1)
            return (b, 0, jnp.where(m == last_mt, col, 0))

        return index

    return pl.pallas_call(
        functools.partial(_attn_inproj_body, last_mt=last_mt),
        grid=(bsz, n_mt, QKV_TILES),
        in_specs=[pl.BlockSpec((1, ATT_MT, d), lambda b, m, n: (b, m, 0)),
                  pl.BlockSpec((1, 1, d), lambda b, m, n: (b, 0, 0)),
                  pl.BlockSpec((1, 1, d), lambda b, m, n: (b, 0, 0)),
                  pl.BlockSpec((d, ATT_TN), lambda b, m, n: (0, n))],
        out_specs=[pl.BlockSpec((1, ATT_MT, ATT_TN), lambda b, m, n: (b, m, n))]
        + [pl.BlockSpec((1, WINDOWS[g], ATT_TN), kv_index(g)) for g in range(3)],
        out_shape=[jax.ShapeDtypeStruct((bsz, length, ATT_QKV), BF16)]
        + [jax.ShapeDtypeStruct((bsz, WINDOWS[g], kv_width), F32) for g in range(3)],
        scratch_shapes=[pltpu.VMEM((ATT_MT, d), BF16), pltpu.VMEM((ATT_TN // LANE_TILE, ATT_MT, LANE_TILE), F32)],
        compiler_params=_cparams(("arbitrary", "arbitrary", "arbitrary")),
        name="attn_inproj_prompt",
    )(x, sc, sh, w_bf)


def _attn_core_body(q0, k0, v0, q1, k1, v1, q2, k2, v2, bias_ref, o_ref, o_scr, l_scr,
                    *, length):
    qkv = ((q0, k0, v0), (q1, k1, v1), (q2, k2, v2))
    lane = lax.broadcasted_iota(jnp.int32, (1, LANE_TILE), 1)
    head_lanes = (lane < HEAD_DIM, lane >= HEAD_DIM)
    nt_dims = (((1,), (1,)), ((), ()))
    for g in range(3):
        win, dil = WINDOWS[g], DILATIONS[g]
        q_ref, k_ref, v_ref = qkv[g]
        for r in range(dil):
            def unit(n, carry, g=g, win=win, dil=dil, r=r, q_ref=q_ref, k_ref=k_ref, v_ref=v_ref):
                cur = pl.multiple_of(n * win + r * BLK, BLK)
                prev = pl.multiple_of(jnp.maximum(n - 1, 0) * win + r * BLK, BLK)
                q = q_ref[0, pl.ds(cur, BLK), :] * QK_SCALE
                k = jnp.concatenate([k_ref[0, pl.ds(prev, BLK), :], k_ref[0, pl.ds(cur, BLK), :]], 0)
                v = jnp.concatenate([v_ref[0, pl.ds(prev, BLK), :], v_ref[0, pl.ds(cur, BLK), :]], 0)
                variant = jnp.minimum(n, 1)
                outs, lses = [], []
                for hh in range(2):
                    qm = jnp.where(head_lanes[hh], q, jnp.zeros_like(q))
                    s = lax.dot_general(qm, k, nt_dims, preferred_element_type=F32)
                    s = s + bias_ref[g, variant, hh]
                    m = jnp.max(jnp.maximum(s[:, :BLK], s[:, BLK:]), axis=-1, keepdims=True)
                    p = jnp.exp(s - m)
                    l = jnp.sum(p[:, :BLK] + p[:, BLK:], axis=-1, keepdims=True)
                    outs.append(jnp.dot(p.astype(BF16), v, preferred_element_type=F32) / l)
                    lses.append(m + jnp.log(l))
                o = jnp.where(head_lanes[0], outs[0], outs[1])
                lse = jnp.where(head_lanes[0], lses[0], lses[1])
                rows = pl.ds(pl.multiple_of(n * win, BLK) + r, BLK, stride=dil)
                o_scr[g, rows, :] = o
                l_scr[g, rows, :] = lse
                return carry

            lax.fori_loop(0, length // win, unit, 0, unroll=min(length // win, UNIT_UNROLL))

    merge_rows = 512
    for c in range(length // merge_rows):
        rows = slice(c * merge_rows, (c + 1) * merge_rows)
        l0, l1, l2 = l_scr[0, rows, :], l_scr[1, rows, :], l_scr[2, rows, :]
        m = jnp.maximum(jnp.maximum(l0, l1), l2)
        e0, e1, e2 = jnp.exp(l0 - m), jnp.exp(l1 - m), jnp.exp(l2 - m)
        num = e0 * o_scr[0, rows, :] + e1 * o_scr[1, rows, :] + e2 * o_scr[2, rows, :]
        o_ref[0, rows, :] = num / (e0 + e1 + e2)


def _prompt_bias():
    h = jnp.arange(ATT_HEADS, dtype=F32) + 1.0
    slopes = jnp.exp2(-ALIBI_BASE_EXP * h / ATT_HEADS)
    a = jnp.arange(BLK)[:, None]
    bk = jnp.arange(2 * BLK)[None, :]
    steps = a - bk + BLK
    ok = (steps >= 0) & (steps <= BLK)
    ok_first = ok & (bk >= BLK)
    tables = []
    for dil in DILATIONS:
        bias = -slopes[:, None, None] * (steps * dil).astype(F32)[None]
        tables.append(jnp.stack([jnp.where(ok_first[None], bias, -jnp.inf),
                                 jnp.where(ok[None], bias, -jnp.inf)], 0))
    return jnp.stack(tables, 0)


def _attn_core_prompt(qkv, bias):
    bsz, length, _ = qkv.shape
    n_hp = ATT_HEADS // 2
    col_tiles = ATT_HEADS * HEAD_DIM // LANE_TILE

    def col_spec(g, part):
        base = (g * 3 + part) * col_tiles
        return pl.BlockSpec((1, length, LANE_TILE), lambda b, hp: (b, 0, base + hp))

    in_specs = [col_spec(g, part) for g in range(3) for part in range(3)]
    in_specs.append(pl.BlockSpec((3, 2, 2, BLK, 2 * BLK), lambda b, hp: (0, 0, hp, 0, 0)))
    return pl.pallas_call(
        functools.partial(_attn_core_body, length=length),
        grid=(bsz, n_hp),
        in_specs=in_specs,
        out_specs=pl.BlockSpec((1, length, LANE_TILE), lambda b, hp: (b, 0, hp)),
        out_shape=jax.ShapeDtypeStruct((bsz, length, ATT_HEADS * HEAD_DIM), F32),
        scratch_shapes=[pltpu.VMEM((3, length, LANE_TILE), F32),
                        pltpu.VMEM((3, length, LANE_TILE), F32)],
        compiler_params=_cparams(("arbitrary", "arbitrary")),
        name="attn_core_prompt",
    )(*([qkv] * 9), bias)


SA_HEADS = 8
SA_ROWS = SA_HEADS * HEAD_DIM
DEC_SEQ = 8
NEG_BIG = -1e30

SA_PASSES = [("g2", 2, 16), ("g1a", 1, 4), ("g1b", 1, 4)]
SA_Q_TILE = {name: i for i, (name, _, _) in enumerate(SA_PASSES)}
SA_NEW = ("g2", "g1a", "g1b", "g0")
SA_K_TILE = {name: len(SA_PASSES) + i for i, name in enumerate(SA_NEW)}
SA_V_TILE = {name: len(SA_PASSES) + len(SA_NEW) + i for i, name in enumerate(SA_NEW)}
SA_TILES = len(SA_PASSES) + 2 * len(SA_NEW)


def _src_lane(part, g, t):
    return (part * 3 + g) * DEC_SEQ + t


def _sample_tables():
    lam = np.arange(LANE_TILE)
    none = np.full(LANE_TILE, -1)

    def tile(src):
        m = np.zeros((LANE_TILE, LANE_TILE), np.float32)
        ok = src >= 0
        m[src[ok], lam[ok]] = 1.0
        return m

    def new_src(part, name):
        if name == "g2":
            return np.where(lam < 8, _src_lane(part, 2, lam % 8), none)
        if name == "g1a":
            return np.where(lam < 4, _src_lane(part, 1, lam % 4), none)
        if name == "g1b":
            return np.where(lam < 4, _src_lane(part, 1, 4 + lam % 4),
                            np.where(lam < 8, _src_lane(part, 1, lam % 4), none))
        return np.where(lam < 8, _src_lane(part, 0, lam % 8), none)

    tiles = []
    for name, g, _ in SA_PASSES:
        if name == "g2":
            tiles.append(tile(np.where(lam % 16 < 8, _src_lane(0, 2, lam % 8), none)))
        elif name == "g1a":
            tiles.append(tile(_src_lane(0, 1, lam % 4)))
        else:
            tiles.append(tile(_src_lane(0, 1, 4 + lam % 4)))
    for part in (1, 2):
        for name in SA_NEW:
            tiles.append(tile(new_src(part, name)))
    place = np.concatenate(tiles, axis=1)

    h = jnp.arange(ATT_HEADS, dtype=F32) + 1.0
    slopes = jnp.exp2(-ALIBI_BASE_EXP * h / ATT_HEADS)[:, None]

    def bias(dist, ok):
        d = jnp.asarray(np.where(ok, dist, 0), F32)[None, :]
        return jnp.where(jnp.asarray(ok)[None, :], -slopes * d, -jnp.inf)

    def table(win, cache_dist, cache_ok, new_dist, new_ok):
        rho = np.arange(win)
        return jnp.concatenate([bias(cache_dist(rho), cache_ok(rho)),
                                bias(new_dist(lam), new_ok(lam))], axis=1)

    zero = lambda l: 0 * l
    b2 = table(2048, lambda r: 2048 - 16 * (r // 16), lambda r: r % 16 < 8, zero, lambda l: l < 8)
    b1a = table(512, lambda r: 512 - 4 * (r // 4), lambda r: r >= 0, zero, lambda l: l < 4)
    b1b = table(512, lambda r: 516 - 4 * (r // 4), lambda r: r >= 4,
                lambda l: np.where(l < 4, 0, 4), lambda l: l < 8)
    b0 = [table(128, lambda r, t=t: 128 + t - r, lambda r, t=t: r >= t,
                lambda l, t=t: t - l, lambda l, t=t: l <= t) for t in range(DEC_SEQ)]
    expand = np.zeros((LANE_TILE, SA_ROWS), np.float32)
    for hh in range(SA_HEADS):
        expand[hh, hh * HEAD_DIM:(hh + 1) * HEAD_DIM] = 1.0
    return (jnp.asarray(place, BF16), b2, jnp.stack([b1a, b1b], 0), jnp.stack(b0, 1),
            jnp.asarray(expand, BF16))


def _class_reduce(x, op, period):
    if period == 1:
        red = jnp.max if op is jnp.maximum else jnp.sum
        return jnp.broadcast_to(red(x, axis=1, keepdims=True), x.shape)
    shift = period
    while shift < LANE_TILE:
        x = op(x, pltpu.roll(x, shift, axis=1))
        shift *= 2
    return x


def _split_dot(a, b_bf):
    hi = a.astype(BF16)
    lo = (a - hi.astype(F32)).astype(BF16)
    return (jnp.dot(hi, b_bf, preferred_element_type=F32)
            + jnp.dot(lo, b_bf, preferred_element_type=F32))


def _attn_sample_body(xt_ref, q0_ref, c0_ref, c1_ref, c2_ref, place_ref, b2_ref, b1_ref, b0_ref,
                      exp_ref, o_ref, pl_scr, s_scr, r_scr, st_scr):
    lt = LANE_TILE
    xt = xt_ref[0].astype(BF16)
    for c in range(SA_TILES):
        pl_scr[c] = jnp.dot(xt, place_ref[:, c * lt:(c + 1) * lt], preferred_element_type=F32)
    st_scr[...] = jnp.zeros_like(st_scr)
    lane = lax.broadcasted_iota(jnp.int32, (1, lt), 1)
    caches = (c0_ref, c1_ref, c2_ref)
    biases = {"g2": lambda: b2_ref[...], "g1a": lambda: b1_ref[0], "g1b": lambda: b1_ref[1]}
    nt_dims = (((1,), (1,)), ((), ()))

    def head_rows(h):
        return slice(h * HEAD_DIM, (h + 1) * HEAD_DIM)

    base = 0
    for slot, (name, g, period) in enumerate(SA_PASSES):
        win = WINDOWS[g]
        n_ct = win // lt
        cache = caches[g]
        q_tile, k_tile, v_tile = SA_Q_TILE[name], SA_K_TILE[name], SA_V_TILE[name]

        def cols(j, base=base):
            return slice(base + j * lt, base + (j + 1) * lt)

        for h in range(SA_HEADS):
            q = pl_scr[q_tile, head_rows(h), :]
            for j in range(n_ct):
                kt = cache[0, 0, 0, h, :, j * lt:(j + 1) * lt]
                s_scr[h:h + 1, cols(j)] = jnp.sum(q * kt, axis=0, keepdims=True)
            kn = pl_scr[k_tile, head_rows(h), :]
            s_scr[h:h + 1, cols(n_ct)] = jnp.sum(q * kn, axis=0, keepdims=True)

        bias = biases[name]()
        tiles = [s_scr[:, cols(j)] * QK_SCALE + bias[:, j * lt:(j + 1) * lt]
                 for j in range(n_ct + 1)]
        m = _class_reduce(functools.reduce(jnp.maximum, tiles), jnp.maximum, period)
        m = jnp.maximum(m, NEG_BIG)
        ps = [jnp.exp(tl - m) for tl in tiles]
        l = _class_reduce(functools.reduce(lambda a, b: a + b, ps), lambda a, b: a + b, period)
        l = jnp.maximum(l, -1.0 / NEG_BIG)
        inv = 1.0 / l
        for j in range(n_ct + 1):
            s_scr[:, cols(j)] = ps[j] * inv
        st_scr[slot, 0:SA_HEADS, :] = m + jnp.log(l)

        for h in range(SA_HEADS):
            acc = s_scr[h:h + 1, cols(n_ct)] * pl_scr[v_tile, head_rows(h), :]
            for j in range(n_ct):
                acc = acc + s_scr[h:h + 1, cols(j)] * cache[0, 0, 1, h, :, j * lt:(j + 1) * lt]
            r_scr[slot, head_rows(h), :] = acc
        base += win + lt

    q0 = q0_ref[0].astype(BF16)
    l0 = jnp.zeros((DEC_SEQ, lt), F32)
    o0_heads = []
    for h in range(SA_HEADS):
        kt = jnp.concatenate([c0_ref[0, 0, 0, h], pl_scr[SA_K_TILE["g0"], head_rows(h), :]], axis=1)
        vt = jnp.concatenate([c0_ref[0, 0, 1, h], pl_scr[SA_V_TILE["g0"], head_rows(h), :]], axis=1)
        s = jnp.dot(q0[:, head_rows(h)], kt.astype(BF16), preferred_element_type=F32)
        s = s * QK_SCALE + b0_ref[h]
        m = jnp.max(s, axis=1, keepdims=True)
        p = jnp.exp(s - m)
        l = jnp.sum(p, axis=1, keepdims=True)
        o0_heads.append(lax.dot_general(p.astype(BF16), vt.astype(BF16), nt_dims,
                                        preferred_element_type=F32) / l)
        l0 = jnp.where(lane == h, m + jnp.log(l), l0)
    o0 = jnp.concatenate(o0_heads, axis=1)

    def by_query(slot):
        rt = r_scr[slot].T
        return functools.reduce(lambda a, b: a + b,
                                [rt[8 * j:8 * (j + 1), :] for j in range(lt // 8)])

    def stats_by_query(slot):
        return st_scr[slot].T[0:DEC_SEQ, :]

    row = lax.broadcasted_iota(jnp.int32, (DEC_SEQ, 1), 0)
    low = row < 4
    o2 = by_query(0)
    oa, ob = by_query(1), by_query(2)
    o1 = jnp.where(low, oa + pltpu.roll(oa, 4, axis=0), ob + pltpu.roll(ob, 4, axis=0))
    l2 = stats_by_query(0)
    l1 = jnp.where(low, stats_by_query(1), pltpu.roll(stats_by_query(2), 4, axis=0))
    mx = jnp.maximum(jnp.maximum(l0, l1), l2)
    e0, e1, e2 = jnp.exp(l0 - mx), jnp.exp(l1 - mx), jnp.exp(l2 - mx)
    den = e0 + e1 + e2
    w = jnp.concatenate([e0 / den, e1 / den, e2 / den], axis=0)
    w = _split_dot(w, exp_ref[...])
    o_ref[0] = (w[0:DEC_SEQ] * o0 + w[DEC_SEQ:2 * DEC_SEQ] * o1 + w[2 * DEC_SEQ:] * o2)


def _attn_core_sample(proj, caches_t, tables, layer):
    dbsz, tlen, _ = proj.shape
    width = ATT_HEADS * HEAD_DIM
    n_hb = ATT_HEADS // SA_HEADS
    xt = proj.reshape(dbsz, tlen, 3, 3, width).transpose(0, 4, 3, 2, 1).reshape(dbsz, width, 9 * tlen)
    xt = jnp.pad(xt, ((0, 0), (0, 0), (0, LANE_TILE - 9 * tlen)))
    place, b2, b1, b0, expand = tables

    def cache_spec(win):
        return pl.BlockSpec((1, 1, 2, SA_HEADS, HEAD_DIM, win),
                            lambda b, hb: (layer, b, 0, hb, 0, 0))

    score_lanes = sum(WINDOWS[g] + LANE_TILE for _, g, _ in SA_PASSES)
    n_pass = len(SA_PASSES)
    return pl.pallas_call(
        _attn_sample_body,
        grid=(dbsz, n_hb),
        in_specs=[pl.BlockSpec((1, SA_ROWS, LANE_TILE), lambda b, hb: (b, hb, 0)),
                  pl.BlockSpec((1, tlen, SA_ROWS), lambda b, hb: (b, 0, hb)),
                  cache_spec(WINDOWS[0]), cache_spec(WINDOWS[1]), cache_spec(WINDOWS[2]),
                  pl.BlockSpec(place.shape, lambda b, hb: (0, 0)),
                  pl.BlockSpec((SA_HEADS, b2.shape[1]), lambda b, hb: (hb, 0)),
                  pl.BlockSpec((2, SA_HEADS, b1.shape[2]), lambda b, hb: (0, hb, 0)),
                  pl.BlockSpec((SA_HEADS, DEC_SEQ, b0.shape[2]), lambda b, hb: (hb, 0, 0)),
                  pl.BlockSpec(expand.shape, lambda b, hb: (0, 0))],
        out_specs=pl.BlockSpec((1, tlen, SA_ROWS), lambda b, hb: (b, 0, hb)),
        out_shape=jax.ShapeDtypeStruct((dbsz, tlen, width), F32),
        scratch_shapes=[pltpu.VMEM((SA_TILES, SA_ROWS, LANE_TILE), F32),
                        pltpu.VMEM((SA_HEADS, score_lanes), F32),
                        pltpu.VMEM((n_pass, SA_ROWS, LANE_TILE), F32),
                        pltpu.VMEM((n_pass, LANE_TILE, LANE_TILE), F32)],
        compiler_params=_cparams(("arbitrary", "arbitrary")),
        name="attn_core_sample",
    )(xt, proj, caches_t[0], caches_t[1], caches_t[2], place, b2, b1, b0, expand)


def kernel(x_prompt, x_sample, c_prompt, c_sample, state_ssm_re, state_ssm_im, cache_kv_w128, cache_kv_w512, cache_kv_w2048, w_ada, b_ada, ln_g, ln_b, ssm_w_in, ssm_a_re, ssm_a_im, ssm_log_dt, ssm_b_re, ssm_b_im, ssm_c_re, ssm_c_im, ssm_d, ssm_w_glu, ssm_b_glu, ssm_w_out, attn_w_in, attn_w_out):
    bsz, seq, d = x_prompt.shape
    dbsz, dec_seq, _ = x_sample.shape
    n_dec = dbsz * dec_seq
    att_width = ATT_HEADS * HEAD_DIM

    pad = (-(bsz + dbsz)) % 8
    c_all = jnp.concatenate([c_prompt, c_sample, jnp.zeros((pad, d), c_prompt.dtype)], 0)
    mod = _ada(c_all, w_ada, b_ada)
    bias_p = _prompt_bias()
    sample_tables = _sample_tables()
    caches_t = [jnp.transpose(c, (0, 1, 3, 4, 5, 2))
                for c in (cache_kv_w128, cache_kv_w512, cache_kv_w2048)]

    xp = x_prompt
    xs = x_sample.reshape(1, n_dec, d)
    ssm_p_re, ssm_p_im, ssm_s_re, ssm_s_im = [], [], [], []
    kv_p = [[], [], []]
    kv_s = [[], [], []]
    for i in range(DEPTH):
        j = i // 2
        mod_p = mod[i, :bsz]
        mod_s = jnp.repeat(mod[i, bsz:bsz + dbsz], dec_seq, axis=0)
        sh_p, sc_p, gt_p = [mod_p[:, None, k * d:(k + 1) * d] for k in range(3)]
        sh_s, sc_s, gt_s = [mod_s[None, :, k * d:(k + 1) * d] for k in range(3)]
        if i % 2 == 0:
            prep = _ssm_prepare(ssm_a_re[j], ssm_a_im[j], ssm_log_dt[j], ssm_b_re[j], ssm_b_im[j],
                                ssm_c_re[j], ssm_c_im[j], ssm_d[j], dec_seq)
            w_in = ssm_w_in[j].astype(BF16)
            w_s, w_z = w_in[:, :d], w_in[:, d:]
            w_glu = ssm_w_glu[j].astype(BF16)
            w_out = ssm_w_out[j].astype(BF16)

            s_p = _modmm(xp, sc_p, sh_p, w_s, 1024, d)
            y_p, h_p = _ssm_core_prompt(s_p, prep, 512)
            hr, hi = _state_from_kt(h_p)
            ssm_p_re.append(hr)
            ssm_p_im.append(hi)
            xp = _tail(xp, sc_p, sh_p, gt_p, y_p, w_z, w_out, ln_g[i], ln_b[i], 512,
                       wglu_bf=w_glu, bglu=ssm_b_glu[j])

            s_s = _modmm(xs, sc_s, sh_s, w_s, n_dec, d)
            h0 = _state_to_kt(state_ssm_re[j], state_ssm_im[j])
            y_s, h_s = _ssm_core_sample(s_s[0], h0, prep, dec_seq)
            hr, hi = _state_from_kt(h_s.transpose(1, 0, 2))
            ssm_s_re.append(hr)
            ssm_s_im.append(hi)
            xs = _tail(xs, sc_s, sh_s, gt_s, y_s[None], w_z, w_out, ln_g[i], ln_b[i], 512,
                       wglu_bf=w_glu, bglu=ssm_b_glu[j])
        else:
            w_in = attn_w_in[j].astype(BF16)
            w_qkv, w_z = w_in[:, :ATT_QKV], w_in[:, ATT_QKV:]
            w_out = attn_w_out[j].astype(BF16)

            qkv, kv0, kv1, kv2 = _attn_inproj_prompt(xp, sc_p, sh_p, w_qkv)
            for g, kv in enumerate((kv0, kv1, kv2)):
                kv_p[g].append(kv.reshape(bsz, WINDOWS[g], 2, ATT_HEADS, HEAD_DIM))
            o_p = _attn_core_prompt(qkv, bias_p)
            xp = _tail(xp, sc_p, sh_p, gt_p, o_p, w_z, w_out, ln_g[i], ln_b[i], 512)

            proj_s = _modmm(xs, sc_s, sh_s, w_qkv, n_dec, 1024)[0].reshape(dbsz, dec_seq, ATT_QKV)
            for g in range(3):
                lo = (3 * g + 1) * att_width
                kv_s[g].append(proj_s[:, :, lo:lo + 2 * att_width]
                               .reshape(dbsz, dec_seq, 2, ATT_HEADS, HEAD_DIM))
            o_s = _attn_core_sample(proj_s, caches_t, sample_tables, j)
            xs = _tail(xs, sc_s, sh_s, gt_s, o_s.reshape(1, n_dec, att_width), w_z, w_out,
                       ln_g[i], ln_b[i], 512)

    return (xp, xs.reshape(dbsz, dec_seq, d),
            jnp.stack(ssm_p_re, 0), jnp.stack(ssm_p_im, 0),
            jnp.stack(kv_p[0], 0), jnp.stack(kv_p[1], 0), jnp.stack(kv_p[2], 0),
            jnp.stack(ssm_s_re, 0), jnp.stack(ssm_s_im, 0),
            jnp.stack(kv_s[0], 0), jnp.stack(kv_s[1], 0), jnp.stack(kv_s[2], 0))
```

```python
import functools

import numpy as np
import jax
import jax.numpy as jnp
from jax import lax
from jax.experimental import pallas as pl
from jax.experimental.pallas import tpu as pltpu

F32 = jnp.float32
BF16 = jnp.bfloat16

D_MODEL = 1024
DEPTH = 4
SSM_GROUPS = 64
SSM_GROUP = 16
SSM_STATE = 64
ATT_HEADS = 16
HEAD_DIM = 64
WINDOWS = (128, 512, 2048)
DILATIONS = (1, 4, 16)
ATT_QKV = 3 * 3 * ATT_HEADS * HEAD_DIM
ALIBI_BASE_EXP = 8.0
DEEPNORM_ALPHA = (2 * DEPTH) ** 0.25
LN_EPS = 1e-5
QK_SCALE = HEAD_DIM ** -0.5

LANE_TILE = 128
BLK = 128
SSM_CHUNK = 128
SSM_CENTER = SSM_CHUNK // 2
SSM_KT = D_MODEL // LANE_TILE
GROUPS_PER_KT = LANE_TILE // SSM_GROUP
KT_STATE = GROUPS_PER_KT * SSM_STATE
VMEM_LIMIT = 56 * 1024 * 1024


def _cparams(sem):
    return pltpu.CompilerParams(dimension_semantics=sem, vmem_limit_bytes=VMEM_LIMIT)


def _ada_body(c_ref, w_ref, b_ref, o_ref):
    c = c_ref[...]
    s = c * jax.nn.sigmoid(c)
    o_ref[0] = jnp.dot(s.astype(BF16), w_ref[0].astype(BF16), preferred_element_type=F32) + b_ref[0]


def _ada(c_all, w_ada, b_ada):
    rows = c_all.shape[0]
    d = D_MODEL
    return pl.pallas_call(
        _ada_body,
        grid=(DEPTH, 3),
        in_specs=[pl.BlockSpec((rows, d), lambda i, n: (0, 0)),
                  pl.BlockSpec((1, d, d), lambda i, n: (i, 0, n)),
                  pl.BlockSpec((1, 1, d), lambda i, n: (i, 0, n))],
        out_specs=pl.BlockSpec((1, rows, d), lambda i, n: (i, 0, n)),
        out_shape=jax.ShapeDtypeStruct((DEPTH, rows, 3 * d), F32),
        compiler_params=_cparams(("arbitrary", "arbitrary")),
        name="ada",
    )(c_all, w_ada, b_ada.reshape(DEPTH, 1, 3 * d))


def _modmm_body(x_ref, sc_ref, sh_ref, w_ref, o_ref, u_scr):
    @pl.when(pl.program_id(2) == 0)
    def _():
        u_scr[...] = (x_ref[0] * (1.0 + sc_ref[0]) + sh_ref[0]).astype(BF16)

    o_ref[0] = jnp.dot(u_scr[...], w_ref[...], preferred_element_type=F32)


def _mod_index(per_token):
    if per_token:
        return lambda b, m, n: (b, m, 0)
    return lambda b, m, n: (b, 0, 0)


def _modmm(x, sc, sh, w_bf, tm, tn):
    bsz, length, d = x.shape
    n_out = w_bf.shape[1]
    per_token = sc.shape[1] != 1
    mod_rows = tm if per_token else 1
    return pl.pallas_call(
        _modmm_body,
        grid=(bsz, length // tm, n_out // tn),
        in_specs=[pl.BlockSpec((1, tm, d), lambda b, m, n: (b, m, 0)),
                  pl.BlockSpec((1, mod_rows, d), _mod_index(per_token)),
                  pl.BlockSpec((1, mod_rows, d), _mod_index(per_token)),
                  pl.BlockSpec((d, tn), lambda b, m, n: (0, n))],
        out_specs=pl.BlockSpec((1, tm, tn), lambda b, m, n: (b, m, n)),
        out_shape=jax.ShapeDtypeStruct((bsz, length, n_out), F32),
        scratch_shapes=[pltpu.VMEM((tm, d), BF16)],
        compiler_params=_cparams(("arbitrary", "arbitrary", "arbitrary")),
        name="mod_proj",
    )(x, sc, sh, w_bf)


def _tail_body(*refs, glu):
    if glu:
        (x_ref, sc_ref, sh_ref, gt_ref, y_ref, wz_ref, wglu_ref, bglu_ref, wout_ref,
         lng_ref, lnb_ref, o_ref) = refs
    else:
        (x_ref, sc_ref, sh_ref, gt_ref, y_ref, wz_ref, wout_ref, lng_ref, lnb_ref, o_ref) = refs
    x = x_ref[0]
    u = (x * (1.0 + sc_ref[0]) + sh_ref[0]).astype(BF16)
    z = jnp.dot(u, wz_ref[...], preferred_element_type=F32)
    y = y_ref[0]
    if glu:
        y = y * jax.nn.sigmoid(
            jnp.dot(y.astype(BF16), wglu_ref[...], preferred_element_type=F32) + bglu_ref[...])
    y = y * (z * jax.nn.sigmoid(z))
    out = jnp.dot(y.astype(BF16), wout_ref[...], preferred_element_type=F32)
    r = DEEPNORM_ALPHA * x + gt_ref[0] * out
    mu = jnp.mean(r, axis=-1, keepdims=True)
    rc = r - mu
    var = jnp.mean(rc * rc, axis=-1, keepdims=True)
    o_ref[0] = rc * lax.rsqrt(var + LN_EPS) * lng_ref[...] + lnb_ref[...]


def _tail(x, sc, sh, gt, y, wz_bf, wout_bf, ln_g, ln_b, tm, wglu_bf=None, bglu=None):
    bsz, length, d = x.shape
    per_token = sc.shape[1] != 1
    mod_rows = tm if per_token else 1
    if per_token:
        mod_idx = lambda b, m: (b, m, 0)
    else:
        mod_idx = lambda b, m: (b, 0, 0)
    row_spec = pl.BlockSpec((1, tm, d), lambda b, m: (b, m, 0))
    mod_spec = pl.BlockSpec((1, mod_rows, d), mod_idx)
    w_spec = pl.BlockSpec((d, d), lambda b, m: (0, 0))
    vec_spec = pl.BlockSpec((1, d), lambda b, m: (0, 0))
    glu = wglu_bf is not None
    operands = [x, sc, sh, gt, y, wz_bf]
    in_specs = [row_spec, mod_spec, mod_spec, mod_spec, row_spec, w_spec]
    if glu:
        operands += [wglu_bf, bglu.reshape(1, d)]
        in_specs += [w_spec, vec_spec]
    operands += [wout_bf, ln_g.reshape(1, d), ln_b.reshape(1, d)]
    in_specs += [w_spec, vec_spec, vec_spec]
    return pl.pallas_call(
        functools.partial(_tail_body, glu=glu),
        grid=(bsz, length // tm),
        in_specs=in_specs,
        out_specs=row_spec,
        out_shape=jax.ShapeDtypeStruct((bsz, length, d), F32),
        compiler_params=_cparams(("arbitrary", "arbitrary")),
        name="tail_glu" if glu else "tail",
    )(*operands)


def _complex_mul(ar, ai, br, bi):
    return ar * br - ai * bi, ar * bi + ai * br


def _prefix_matmul(l_mat, g):
    hi = g.astype(BF16)
    lo = (g - hi.astype(F32)).astype(BF16)
    return (jnp.dot(l_mat, hi, preferred_element_type=F32)
            + jnp.dot(l_mat, lo, preferred_element_type=F32))


def _ssm_scaled_inputs(s_c, bdb, tn):
    ks = KT_STATE
    bu = jnp.dot(s_c.astype(BF16), bdb, preferred_element_type=F32)
    return _complex_mul(tn[:, :ks], tn[:, ks:], bu[:, :ks], bu[:, ks:])


def _ssm_outputs(s_c, p_re, p_im, tp, bdc, dsk):
    ks = KT_STATE
    h_re, h_im = _complex_mul(tp[:, :ks], tp[:, ks:], p_re, p_im)
    y = (jnp.dot(h_re.astype(BF16), bdc[:ks], preferred_element_type=F32)
         + jnp.dot(h_im.astype(BF16), bdc[ks:], preferred_element_type=F32))
    return jax.nn.gelu(y + dsk * s_c), h_re, h_im


def _ssm_prompt_body(s_ref, bdb_ref, bdc_ref, tn_ref, tp_ref, l_ref, cm_ref, d_ref,
                     y_ref, h_ref, carry_scr, *, n_chunks):
    ks = KT_STATE
    t = SSM_CHUNK

    @pl.when(pl.program_id(2) == 0)
    def _():
        carry_scr[...] = jnp.zeros_like(carry_scr)

    l_mat = l_ref[...]
    tn = tn_ref[0]
    tp = tp_ref[0]
    cm = cm_ref[0]
    bdb = bdb_ref[0]
    bdc = bdc_ref[0]
    dsk = d_ref[...]
    sums = []
    for ci in range(n_chunks):
        s_c = s_ref[0, ci * t:(ci + 1) * t, :]
        g_re, g_im = _ssm_scaled_inputs(s_c, bdb, tn)
        sums.append((_prefix_matmul(l_mat, g_re), _prefix_matmul(l_mat, g_im)))
    hp = carry_scr[0:1, :]
    carried = []
    for p_re, p_im in sums:
        c_re, c_im = _complex_mul(cm[:, :ks], cm[:, ks:], hp[:, :ks], hp[:, ks:])
        carried.append((c_re, c_im))
        e_re, e_im = _complex_mul(tp[t - 1:t, :ks], tp[t - 1:t, ks:],
                                  p_re[t - 1:t] + c_re, p_im[t - 1:t] + c_im)
        hp = jnp.concatenate([e_re, e_im], axis=1)
    for ci in range(n_chunks):
        s_c = s_ref[0, ci * t:(ci + 1) * t, :]
        (p_re, p_im), (c_re, c_im) = sums[ci], carried[ci]
        y, _, _ = _ssm_outputs(s_c, p_re + c_re, p_im + c_im, tp, bdc, dsk)
        y_ref[0, ci * t:(ci + 1) * t, :] = y
    carry_scr[0:1, :] = hp
    h_ref[0, 0] = jnp.broadcast_to(hp, (8, 2 * ks))


def _ssm_core_prompt(s, prep, tt):
    bsz, length, d = s.shape
    t = SSM_CHUNK
    w2 = 2 * KT_STATE
    y, h = pl.pallas_call(
        functools.partial(_ssm_prompt_body, n_chunks=tt // t),
        grid=(bsz, SSM_KT, length // tt),
        in_specs=[pl.BlockSpec((1, tt, LANE_TILE), lambda b, k, c: (b, c, k)),
                  pl.BlockSpec((1, LANE_TILE, w2), lambda b, k, c: (k, 0, 0)),
                  pl.BlockSpec((1, w2, LANE_TILE), lambda b, k, c: (k, 0, 0)),
                  pl.BlockSpec((1, t, w2), lambda b, k, c: (k, 0, 0)),
                  pl.BlockSpec((1, t, w2), lambda b, k, c: (k, 0, 0)),
                  pl.BlockSpec((t, t), lambda b, k, c: (0, 0)),
                  pl.BlockSpec((1, 1, w2), lambda b, k, c: (k, 0, 0)),
                  pl.BlockSpec((1, LANE_TILE), lambda b, k, c: (0, k))],
        out_specs=[pl.BlockSpec((1, tt, LANE_TILE), lambda b, k, c: (b, c, k)),
                   pl.BlockSpec((1, 1, 8, w2), lambda b, k, c: (b, k, 0, 0))],
        out_shape=[jax.ShapeDtypeStruct((bsz, length, d), F32),
                   jax.ShapeDtypeStruct((bsz, SSM_KT, 8, w2), F32)],
        scratch_shapes=[pltpu.VMEM((8, w2), F32)],
        compiler_params=_cparams(("arbitrary", "arbitrary", "arbitrary")),
        name="ssm_core_prompt",
    )(s, prep["bdb"], prep["bdc"], prep["tn_p"], prep["tp_p"], prep["l_p"], prep["cm_p"],
      prep["dsk"])
    return y, h[:, :, 0, :]


def _lane_tiles_store(scr, v):
    for c in range(v.shape[1] // LANE_TILE):
        scr[c] = v[:, c * LANE_TILE:(c + 1) * LANE_TILE]


def _lane_tiles_load(scr, rows):
    return jnp.concatenate([scr[c, rows, :] for c in range(scr.shape[0])], axis=1)


def _ssm_sample_body(s_ref, h0_ref, bdb_ref, bdc_ref, tn_ref, tp_ref, l_ref, cm_ref, d_ref,
                     y_ref, h_ref, gre_scr, gim_scr, *, dec_seq):
    ks = KT_STATE
    nb = SSM_CHUNK // dec_seq
    cm = cm_ref[0]
    h0 = h0_ref[0]
    s_c = s_ref[...]
    g_re, g_im = _ssm_scaled_inputs(s_c, bdb_ref[0], tn_ref[0])
    c_re, c_im = _complex_mul(cm[:, :ks], cm[:, ks:], h0[:, :ks], h0[:, ks:])
    first = pl.ds(0, nb, stride=dec_seq)
    _lane_tiles_store(gre_scr, g_re)
    _lane_tiles_store(gim_scr, g_im)
    for c in range(ks // LANE_TILE):
        cols = slice(c * LANE_TILE, (c + 1) * LANE_TILE)
        gre_scr[c, first, :] = gre_scr[c, first, :] + c_re[:, cols]
        gim_scr[c, first, :] = gim_scr[c, first, :] + c_im[:, cols]
    l_mat = l_ref[...]
    p_re = _prefix_matmul(l_mat, _lane_tiles_load(gre_scr, slice(None)))
    p_im = _prefix_matmul(l_mat, _lane_tiles_load(gim_scr, slice(None)))
    y, h_re, h_im = _ssm_outputs(s_c, p_re, p_im, tp_ref[0], bdc_ref[0], d_ref[...])
    y_ref[...] = y
    _lane_tiles_store(gre_scr, h_re)
    _lane_tiles_store(gim_scr, h_im)
    last = pl.ds(dec_seq - 1, nb, stride=dec_seq)
    h_ref[0, :, :ks] = _lane_tiles_load(gre_scr, last)
    h_ref[0, :, ks:] = _lane_tiles_load(gim_scr, last)


def _ssm_core_sample(s, h0, prep, dec_seq):
    rows, d = s.shape
    t = SSM_CHUNK
    nb = t // dec_seq
    w2 = 2 * KT_STATE
    return pl.pallas_call(
        functools.partial(_ssm_sample_body, dec_seq=dec_seq),
        grid=(SSM_KT, rows // t),
        in_specs=[pl.BlockSpec((t, LANE_TILE), lambda k, c: (c, k)),
                  pl.BlockSpec((1, nb, w2), lambda k, c: (k, c, 0)),
                  pl.BlockSpec((1, LANE_TILE, w2), lambda k, c: (k, 0, 0)),
                  pl.BlockSpec((1, w2, LANE_TILE), lambda k, c: (k, 0, 0)),
                  pl.BlockSpec((1, t, w2), lambda k, c: (k, 0, 0)),
                  pl.BlockSpec((1, t, w2), lambda k, c: (k, 0, 0)),
                  pl.BlockSpec((t, t), lambda k, c: (0, 0)),
                  pl.BlockSpec((1, 1, w2), lambda k, c: (k, 0, 0)),
                  pl.BlockSpec((1, LANE_TILE), lambda k, c: (0, k))],
        out_specs=[pl.BlockSpec((t, LANE_TILE), lambda k, c: (c, k)),
                   pl.BlockSpec((1, nb, w2), lambda k, c: (k, c, 0))],
        out_shape=[jax.ShapeDtypeStruct((rows, d), F32),
                   jax.ShapeDtypeStruct((SSM_KT, rows // dec_seq, w2), F32)],
        scratch_shapes=[pltpu.VMEM((KT_STATE // LANE_TILE, t, LANE_TILE), F32)] * 2,
        compiler_params=_cparams(("arbitrary", "arbitrary")),
        name="ssm_core_sample",
    )(s, h0, prep["bdb"], prep["bdc"], prep["tn_s"], prep["tp_s"], prep["l_s"], prep["cm_s"],
      prep["dsk"])


def _to_kt_lanes(re, im):
    rows = re.shape[0]

    def tile(v):
        return v.reshape(rows, SSM_KT, KT_STATE).transpose(1, 0, 2)

    return jnp.concatenate([tile(re), tile(im)], axis=-1)


def _ssm_prepare(a_re, a_im, log_dt, b_re, b_im, c_re, c_im, d_skip, dec_seq):
    dt = jnp.exp(log_dt)[:, None]
    lr, li = a_re, a_im
    mag = jnp.exp(lr * dt)
    abar_re = mag * jnp.cos(li * dt)
    abar_im = mag * jnp.sin(li * dt)
    den = lr * lr + li * li
    xr = abar_re - 1.0
    g_re = (xr * lr + abar_im * li) / den
    g_im = (abar_im * lr - xr * li) / den
    bbar_re = g_re[..., None] * b_re - g_im[..., None] * b_im
    bbar_im = g_re[..., None] * b_im + g_im[..., None] * b_re

    eye = jnp.eye(GROUPS_PER_KT, dtype=a_re.dtype)

    def in_map(bb):
        bb = bb.reshape(SSM_KT, GROUPS_PER_KT, SSM_STATE, SSM_GROUP)
        m = jnp.einsum("ab,kbpc->kacbp", eye, bb)
        return m.reshape(SSM_KT, LANE_TILE, KT_STATE)

    def out_map(cc):
        cc = cc.reshape(SSM_KT, GROUPS_PER_KT, SSM_GROUP, SSM_STATE)
        m = jnp.einsum("ab,kbcp->kapbc", eye, cc)
        return m.reshape(SSM_KT, KT_STATE, LANE_TILE)

    bdb = jnp.concatenate([in_map(bbar_re), in_map(bbar_im)], axis=-1).astype(BF16)
    bdc = jnp.concatenate([out_map(c_re), out_map(-c_im)], axis=1).astype(BF16)

    t = SSM_CHUNK
    c = SSM_CENTER
    log_mag = (lr * dt).reshape(SSM_KT, 1, KT_STATE)
    phase = (li * dt).reshape(SSM_KT, 1, KT_STATE)

    def power_rows(exps):
        k = jnp.asarray(exps, dtype=a_re.dtype).reshape(1, len(exps), 1)
        mag_k = jnp.exp(k * log_mag)
        return jnp.concatenate([mag_k * jnp.cos(k * phase), mag_k * jnp.sin(k * phase)], axis=-1)

    tn_p = power_rows([c - j for j in range(t)])
    tp_p = power_rows([i - c for i in range(t)])
    cm_p = power_rows([c + 1])
    tn_s = power_rows([-(r % dec_seq) for r in range(t)])
    tp_s = power_rows([r % dec_seq for r in range(t)])
    cm_s = power_rows([1])

    idx = jnp.arange(t)
    l_p = (idx[:, None] >= idx[None, :]).astype(BF16)
    l_s = ((idx[:, None] >= idx[None, :])
           & (idx[:, None] // dec_seq == idx[None, :] // dec_seq)).astype(BF16)
    return dict(bdb=bdb, bdc=bdc, tn_p=tn_p, tp_p=tp_p, cm_p=cm_p, tn_s=tn_s, tp_s=tp_s,
                cm_s=cm_s, l_p=l_p, l_s=l_s, dsk=d_skip.reshape(1, D_MODEL))


def _state_to_kt(h_re, h_im):
    return _to_kt_lanes(h_re, h_im)


def _state_from_kt(h):
    bsz = h.shape[0]
    re = h[:, :, :KT_STATE].reshape(bsz, SSM_GROUPS, SSM_STATE)
    im = h[:, :, KT_STATE:].reshape(bsz, SSM_GROUPS, SSM_STATE)
    return re, im


ATT_MT = 2048
ATT_TN = 512
QKV_TILES = ATT_QKV // ATT_TN
TILES_PER_GROUP = QKV_TILES // 3
KV_FIRST_TILE = TILES_PER_GROUP // 3
KV_TILES = 2 * TILES_PER_GROUP // 3
UNIT_UNROLL = 8


def _attn_inproj_body(x_ref, sc_ref, sh_ref, w_ref, qkv_ref, kv0_ref, kv1_ref, kv2_ref,
                      u_scr, acc_scr, *, last_mt):
    mt = pl.program_id(1)
    nt = pl.program_id(2)

    @pl.when(nt == 0)
    def _():
        u_scr[...] = (x_ref[0] * (1.0 + sc_ref[0]) + sh_ref[0]).astype(BF16)

    _lane_tiles_store(acc_scr, jnp.dot(u_scr[...], w_ref[...], preferred_element_type=F32))
    n_ct = ATT_TN // LANE_TILE
    group = nt // TILES_PER_GROUP
    kv_refs = (kv0_ref, kv1_ref, kv2_ref)
    for g in range(3):
        win, dil = WINDOWS[g], DILATIONS[g]

        @pl.when(group == g)
        def _(win=win, dil=dil):
            for c in range(n_ct):
                cols = slice(c * LANE_TILE, (c + 1) * LANE_TILE)
                if dil == 1:
                    qkv_ref[0, :, cols] = acc_scr[c].astype(BF16)
                    continue
                for n in range(ATT_MT // win):
                    for r in range(dil):
                        src = acc_scr[c, pl.ds(n * win + r, BLK, stride=dil), :]
                        dst = n * win + r * BLK
                        qkv_ref[0, dst:dst + BLK, cols] = src.astype(BF16)

        first = g * TILES_PER_GROUP + KV_FIRST_TILE

        @pl.when((mt == last_mt) & (nt >= first) & (nt < first + KV_TILES))
        def _(win=win, kv_ref=kv_refs[g]):
            for c in range(n_ct):
                kv_ref[0, :, c * LANE_TILE:(c + 1) * LANE_TILE] = acc_scr[c, ATT_MT - win:, :]


def _attn_inproj_prompt(x, sc, sh, w_bf):
    bsz, length, d = x.shape
    n_mt = length // ATT_MT
    last_mt = n_mt - 1
    kv_width = KV_TILES * ATT_TN

    def kv_index(g):
        first = g * TILES_PER_GROUP + KV_FIRST_TILE

        def index(b, m, n):
            col = jnp.clip(n - first, 0, KV_TILES - 1)
            return (b, 0, jnp.where(m == last_mt, col, 0))

        return index

    return pl.pallas_call(
        functools.partial(_attn_inproj_body, last_mt=last_mt),
        grid=(bsz, n_mt, QKV_TILES),
        in_specs=[pl.BlockSpec((1, ATT_MT, d), lambda b, m, n: (b, m, 0)),
                  pl.BlockSpec((1, 1, d), lambda b, m, n: (b, 0, 0)),
                  pl.BlockSpec((1, 1, d), lambda b, m, n: (b, 0, 0)),
                  pl.BlockSpec((d, ATT_TN), lambda b, m, n: (0, n))],
        out_specs=[pl.BlockSpec((1, ATT_MT, ATT_TN), lambda b, m, n: (b, m, n))]
        + [pl.BlockSpec((1, WINDOWS[g], ATT_TN), kv_index(g)) for g in range(3)],
        out_shape=[jax.ShapeDtypeStruct((bsz, length, ATT_QKV), BF16)]
        + [jax.ShapeDtypeStruct((bsz, WINDOWS[g], kv_width), F32) for g in range(3)],
        scratch_shapes=[pltpu.VMEM((ATT_MT, d), BF16), pltpu.VMEM((ATT_TN // LANE_TILE, ATT_MT, LANE_TILE), F32)],
        compiler_params=_cparams(("arbitrary", "arbitrary", "arbitrary")),
        name="attn_inproj_prompt",
    )(x, sc, sh, w_bf)


def _attn_core_body(q0, k0, v0, q1, k1, v1, q2, k2, v2, bias_ref, o_ref, o_scr, l_scr,
                    *, length):
    qkv = ((q0, k0, v0), (q1, k1, v1), (q2, k2, v2))
    lane = lax.broadcasted_iota(jnp.int32, (1, LANE_TILE), 1)
    head_lanes = (lane < HEAD_DIM, lane >= HEAD_DIM)
    nt_dims = (((1,), (1,)), ((), ()))
    for g in range(3):
        win, dil = WINDOWS[g], DILATIONS[g]
        q_ref, k_ref, v_ref = qkv[g]
        for r in range(dil):
            def unit(n, carry, g=g, win=win, dil=dil, r=r, q_ref=q_ref, k_ref=k_ref, v_ref=v_ref):
                cur = pl.multiple_of(n * win + r * BLK, BLK)
                prev = pl.multiple_of(jnp.maximum(n - 1, 0) * win + r * BLK, BLK)
                q = q_ref[0, pl.ds(cur, BLK), :] * QK_SCALE
                k = jnp.concatenate([k_ref[0, pl.ds(prev, BLK), :], k_ref[0, pl.ds(cur, BLK), :]], 0)
                v = jnp.concatenate([v_ref[0, pl.ds(prev, BLK), :], v_ref[0, pl.ds(cur, BLK), :]], 0)
                variant = jnp.minimum(n, 1)
                outs, lses = [], []
                for hh in range(2):
                    qm = jnp.where(head_lanes[hh], q, jnp.zeros_like(q))
                    s = lax.dot_general(qm, k, nt_dims, preferred_element_type=F32)
                    s = s + bias_ref[g, variant, hh]
                    m = jnp.max(jnp.maximum(s[:, :BLK], s[:, BLK:]), axis=-1, keepdims=True)
                    p = jnp.exp(s - m)
                    l = jnp.sum(p[:, :BLK] + p[:, BLK:], axis=-1, keepdims=True)
                    outs.append(jnp.dot(p.astype(BF16), v, preferred_element_type=F32) / l)
                    lses.append(m + jnp.log(l))
                o = jnp.where(head_lanes[0], outs[0], outs[1])
                lse = jnp.where(head_lanes[0], lses[0], lses[1])
                rows = pl.ds(pl.multiple_of(n * win, BLK) + r, BLK, stride=dil)
                o_scr[g, rows, :] = o
                l_scr[g, rows, :] = lse
                return carry

            lax.fori_loop(0, length // win, unit, 0, unroll=min(length // win, UNIT_UNROLL))

    merge_rows = 512
    for c in range(length // merge_rows):
        rows = slice(c * merge_rows, (c + 1) * merge_rows)
        l0, l1, l2 = l_scr[0, rows, :], l_scr[1, rows, :], l_scr[2, rows, :]
        m = jnp.maximum(jnp.maximum(l0, l1), l2)
        e0, e1, e2 = jnp.exp(l0 - m), jnp.exp(l1 - m), jnp.exp(l2 - m)
        num = e0 * o_scr[0, rows, :] + e1 * o_scr[1, rows, :] + e2 * o_scr[2, rows, :]
        o_ref[0, rows, :] = num / (e0 + e1 + e2)


def _prompt_bias():
    h = jnp.arange(ATT_HEADS, dtype=F32) + 1.0
    slopes = jnp.exp2(-ALIBI_BASE_EXP * h / ATT_HEADS)
    a = jnp.arange(BLK)[:, None]
    bk = jnp.arange(2 * BLK)[None, :]
    steps = a - bk + BLK
    ok = (steps >= 0) & (steps <= BLK)
    ok_first = ok & (bk >= BLK)
    tables = []
    for dil in DILATIONS:
        bias = -slopes[:, None, None] * (steps * dil).astype(F32)[None]
        tables.append(jnp.stack([jnp.where(ok_first[None], bias, -jnp.inf),
                                 jnp.where(ok[None], bias, -jnp.inf)], 0))
    return jnp.stack(tables, 0)


def _attn_core_prompt(qkv, bias):
    bsz, length, _ = qkv.shape
    n_hp = ATT_HEADS // 2
    col_tiles = ATT_HEADS * HEAD_DIM // LANE_TILE

    def col_spec(g, part):
        base = (g * 3 + part) * col_tiles
        return pl.BlockSpec((1, length, LANE_TILE), lambda b, hp: (b, 0, base + hp))

    in_specs = [col_spec(g, part) for g in range(3) for part in range(3)]
    in_specs.append(pl.BlockSpec((3, 2, 2, BLK, 2 * BLK), lambda b, hp: (0, 0, hp, 0, 0)))
    return pl.pallas_call(
        functools.partial(_attn_core_body, length=length),
        grid=(bsz, n_hp),
        in_specs=in_specs,
        out_specs=pl.BlockSpec((1, length, LANE_TILE), lambda b, hp: (b, 0, hp)),
        out_shape=jax.ShapeDtypeStruct((bsz, length, ATT_HEADS * HEAD_DIM), F32),
        scratch_shapes=[pltpu.VMEM((3, length, LANE_TILE), F32),
                        pltpu.VMEM((3, length, LANE_TILE), F32)],
        compiler_params=_cparams(("arbitrary", "arbitrary")),
        name="attn_core_prompt",
    )(*([qkv] * 9), bias)


SA_HEADS = 8
SA_ROWS = SA_HEADS * HEAD_DIM
DEC_SEQ = 8
SA_SLOTS = 4
NEG_BIG = -1e30

SA_PASSES = [("g2", 2, 16), ("g1a", 1, 4), ("g1b", 1, 4)]
SA_Q_TILE = {name: i for i, (name, _, _) in enumerate(SA_PASSES)}
SA_NEW = ("g2", "g1a", "g1b", "g0")
SA_K_TILE = {name: len(SA_PASSES) + i for i, name in enumerate(SA_NEW)}
SA_V_TILE = {name: len(SA_PASSES) + len(SA_NEW) + i for i, name in enumerate(SA_NEW)}
SA_TILES = len(SA_PASSES) + 2 * len(SA_NEW)


def _src_lane(part, g, t):
    return (part * SA_SLOTS + g) * DEC_SEQ + t


def _sample_tables():
    lam = np.arange(LANE_TILE)
    none = np.full(LANE_TILE, -1)

    def tile(src):
        m = np.zeros((LANE_TILE, LANE_TILE), np.float32)
        ok = src >= 0
        m[src[ok], lam[ok]] = 1.0
        return m

    def new_src(part, name):
        if name == "g2":
            return np.where(lam < 8, _src_lane(part, 2, lam % 8), none)
        if name == "g1a":
            return np.where(lam < 4, _src_lane(part, 1, lam % 4), none)
        if name == "g1b":
            return np.where(lam < 4, _src_lane(part, 1, 4 + lam % 4),
                            np.where(lam < 8, _src_lane(part, 1, lam % 4), none))
        return np.where(lam < 8, _src_lane(part, 0, lam % 8), none)

    tiles = []
    for name, g, _ in SA_PASSES:
        if name == "g2":
            tiles.append(tile(np.where(lam % 16 < 8, _src_lane(0, 2, lam % 8), none)))
        elif name == "g1a":
            tiles.append(tile(_src_lane(0, 1, lam % 4)))
        else:
            tiles.append(tile(_src_lane(0, 1, 4 + lam % 4)))
    for part in (1, 2):
        for name in SA_NEW:
            tiles.append(tile(new_src(part, name)))
    place = np.concatenate(tiles, axis=1)

    h = jnp.arange(ATT_HEADS, dtype=F32) + 1.0
    slopes = jnp.exp2(-ALIBI_BASE_EXP * h / ATT_HEADS)[:, None]

    def bias(dist, ok):
        d = jnp.asarray(np.where(ok, dist, 0), F32)[None, :]
        return jnp.where(jnp.asarray(ok)[None, :], -slopes * d, -jnp.inf)

    def table(win, cache_dist, cache_ok, new_dist, new_ok):
        rho = np.arange(win)
        return jnp.concatenate([bias(cache_dist(rho), cache_ok(rho)),
                                bias(new_dist(lam), new_ok(lam))], axis=1)

    zero = lambda l: 0 * l
    b2 = table(2048, lambda r: 2048 - 16 * (r // 16), lambda r: r % 16 < 8, zero, lambda l: l < 8)
    b1a = table(512, lambda r: 512 - 4 * (r // 4), lambda r: r >= 0, zero, lambda l: l < 4)
    b1b = table(512, lambda r: 516 - 4 * (r // 4), lambda r: r >= 4,
                lambda l: np.where(l < 4, 0, 4), lambda l: l < 8)
    b0 = [table(128, lambda r, t=t: 128 + t - r, lambda r, t=t: r >= t,
                lambda l, t=t: t - l, lambda l, t=t: l <= t) for t in range(DEC_SEQ)]
    expand = np.zeros((LANE_TILE, SA_ROWS), np.float32)
    for hh in range(SA_HEADS):
        expand[hh, hh * HEAD_DIM:(hh + 1) * HEAD_DIM] = 1.0
    return (jnp.asarray(place, BF16), b2, jnp.stack([b1a, b1b], 0), jnp.stack(b0, 1),
            jnp.asarray(expand, BF16))


def _class_reduce(x, op, period):
    if period == 1:
        red = jnp.max if op is jnp.maximum else jnp.sum
        return jnp.broadcast_to(red(x, axis=1, keepdims=True), x.shape)
    shift = period
    while shift < LANE_TILE:
        x = op(x, pltpu.roll(x, shift, axis=1))
        shift *= 2
    return x


def _split_dot(a, b_bf):
    hi = a.astype(BF16)
    lo = (a - hi.astype(F32)).astype(BF16)
    return (jnp.dot(hi, b_bf, preferred_element_type=F32)
            + jnp.dot(lo, b_bf, preferred_element_type=F32))


def _attn_sample_body(xt_ref, q0_ref, c0_ref, c1_ref, c2_ref, place_ref, b2_ref, b1_ref, b0_ref,
                      exp_ref, o_ref, pl_scr, s_scr, r_scr, st_scr):
    lt = LANE_TILE
    xt = xt_ref[0].astype(BF16)
    for c in range(SA_TILES):
        pl_scr[c] = jnp.dot(xt, place_ref[:, c * lt:(c + 1) * lt], preferred_element_type=F32)
    st_scr[...] = jnp.zeros_like(st_scr)
    lane = lax.broadcasted_iota(jnp.int32, (1, lt), 1)
    caches = (c0_ref, c1_ref, c2_ref)
    biases = {"g2": lambda: b2_ref[...], "g1a": lambda: b1_ref[0], "g1b": lambda: b1_ref[1]}
    nt_dims = (((1,), (1,)), ((), ()))

    def head_rows(h):
        return slice(h * HEAD_DIM, (h + 1) * HEAD_DIM)

    base = 0
    for slot, (name, g, period) in enumerate(SA_PASSES):
        win = WINDOWS[g]
        n_ct = win // lt
        cache = caches[g]
        q_tile, k_tile, v_tile = SA_Q_TILE[name], SA_K_TILE[name], SA_V_TILE[name]

        def cols(j, base=base):
            return slice(base + j * lt, base + (j + 1) * lt)

        for h in range(SA_HEADS):
            q = pl_scr[q_tile, head_rows(h), :]
            for j in range(n_ct):
                kt = cache[0, 0, 0, h, :, j * lt:(j + 1) * lt]
                s_scr[h:h + 1, cols(j)] = jnp.sum(q * kt, axis=0, keepdims=True)
            kn = pl_scr[k_tile, head_rows(h), :]
            s_scr[h:h + 1, cols(n_ct)] = jnp.sum(q * kn, axis=0, keepdims=True)

        bias = biases[name]()
        tiles = [s_scr[:, cols(j)] * QK_SCALE + bias[:, j * lt:(j + 1) * lt]
                 for j in range(n_ct + 1)]
        m = _class_reduce(functools.reduce(jnp.maximum, tiles), jnp.maximum, period)
        m = jnp.maximum(m, NEG_BIG)
        ps = [jnp.exp(tl - m) for tl in tiles]
        l = _class_reduce(functools.reduce(lambda a, b: a + b, ps), lambda a, b: a + b, period)
        l = jnp.maximum(l, -1.0 / NEG_BIG)
        inv = 1.0 / l
        for j in range(n_ct + 1):
            s_scr[:, cols(j)] = ps[j] * inv
        st_scr[slot, 0:SA_HEADS, :] = m + jnp.log(l)

        for h in range(SA_HEADS):
            acc = s_scr[h:h + 1, cols(n_ct)] * pl_scr[v_tile, head_rows(h), :]
            for j in range(n_ct):
                acc = acc + s_scr[h:h + 1, cols(j)] * cache[0, 0, 1, h, :, j * lt:(j + 1) * lt]
            r_scr[slot, head_rows(h), :] = acc
        base += win + lt

    q0 = q0_ref[0].astype(BF16)
    l0 = jnp.zeros((DEC_SEQ, lt), F32)
    o0_heads = []
    for h in range(SA_HEADS):
        kt = jnp.concatenate([c0_ref[0, 0, 0, h], pl_scr[SA_K_TILE["g0"], head_rows(h), :]], axis=1)
        vt = jnp.concatenate([c0_ref[0, 0, 1, h], pl_scr[SA_V_TILE["g0"], head_rows(h), :]], axis=1)
        s = jnp.dot(q0[:, head_rows(h)], kt.astype(BF16), preferred_element_type=F32)
        s = s * QK_SCALE + b0_ref[h]
        m = jnp.max(s, axis=1, keepdims=True)
        p = jnp.exp(s - m)
        l = jnp.sum(p, axis=1, keepdims=True)
        o0_heads.append(lax.dot_general(p.astype(BF16), vt.astype(BF16), nt_dims,
                                        preferred_element_type=F32) / l)
        l0 = jnp.where(lane == h, m + jnp.log(l), l0)
    o0 = jnp.concatenate(o0_heads, axis=1)

    def by_query(slot):
        rt = r_scr[slot].T
        return functools.reduce(lambda a, b: a + b,
                                [rt[8 * j:8 * (j + 1), :] for j in range(lt // 8)])

    def stats_by_query(slot):
        return st_scr[slot].T[0:DEC_SEQ, :]

    row = lax.broadcasted_iota(jnp.int32, (DEC_SEQ, 1), 0)
    low = row < 4
    o2 = by_query(0)
    oa, ob = by_query(1), by_query(2)
    o1 = jnp.where(low, oa + pltpu.roll(oa, 4, axis=0), ob + pltpu.roll(ob, 4, axis=0))
    l2 = stats_by_query(0)
    l1 = jnp.where(low, stats_by_query(1), pltpu.roll(stats_by_query(2), 4, axis=0))
    mx = jnp.maximum(jnp.maximum(l0, l1), l2)
    e0, e1, e2 = jnp.exp(l0 - mx), jnp.exp(l1 - mx), jnp.exp(l2 - mx)
    den = e0 + e1 + e2
    w = jnp.concatenate([e0 / den, e1 / den, e2 / den], axis=0)
    w = _split_dot(w, exp_ref[...])
    o_ref[0] = (w[0:DEC_SEQ] * o0 + w[DEC_SEQ:2 * DEC_SEQ] * o1 + w[2 * DEC_SEQ:] * o2)


def _attn_core_sample(proj, caches_t, tables, layer):
    dbsz, tlen, _ = proj.shape
    width = ATT_HEADS * HEAD_DIM
    n_hb = ATT_HEADS // SA_HEADS
    xt = jnp.pad(proj.reshape(dbsz, tlen, 3, 3, width),
                 ((0, 0), (0, 0), (0, SA_SLOTS - 3), (0, SA_SLOTS - 3), (0, 0)))
    xt = xt.transpose(0, 4, 3, 2, 1).reshape(dbsz, width, SA_SLOTS * SA_SLOTS * tlen)
    place, b2, b1, b0, expand = tables

    def cache_spec(win):
        return pl.BlockSpec((1, 1, 2, SA_HEADS, HEAD_DIM, win),
                            lambda b, hb: (layer, b, 0, hb, 0, 0))

    score_lanes = sum(WINDOWS[g] + LANE_TILE for _, g, _ in SA_PASSES)
    n_pass = len(SA_PASSES)
    return pl.pallas_call(
        _attn_sample_body,
        grid=(dbsz, n_hb),
        in_specs=[pl.BlockSpec((1, SA_ROWS, LANE_TILE), lambda b, hb: (b, hb, 0)),
                  pl.BlockSpec((1, tlen, SA_ROWS), lambda b, hb: (b, 0, hb)),
                  cache_spec(WINDOWS[0]), cache_spec(WINDOWS[1]), cache_spec(WINDOWS[2]),
                  pl.BlockSpec(place.shape, lambda b, hb: (0, 0)),
                  pl.BlockSpec((SA_HEADS, b2.shape[1]), lambda b, hb: (hb, 0)),
                  pl.BlockSpec((2, SA_HEADS, b1.shape[2]), lambda b, hb: (0, hb, 0)),
                  pl.BlockSpec((SA_HEADS, DEC_SEQ, b0.shape[2]), lambda b, hb: (hb, 0, 0)),
                  pl.BlockSpec(expand.shape, lambda b, hb: (0, 0))],
        out_specs=pl.BlockSpec((1, tlen, SA_ROWS), lambda b, hb: (b, 0, hb)),
        out_shape=jax.ShapeDtypeStruct((dbsz, tlen, width), F32),
        scratch_shapes=[pltpu.VMEM((SA_TILES, SA_ROWS, LANE_TILE), F32),
                        pltpu.VMEM((SA_HEADS, score_lanes), F32),
                        pltpu.VMEM((n_pass, SA_ROWS, LANE_TILE), F32),
                        pltpu.VMEM((n_pass, LANE_TILE, LANE_TILE), F32)],
        compiler_params=_cparams(("arbitrary", "arbitrary")),
        name="attn_core_sample",
    )(xt, proj, caches_t[0], caches_t[1], caches_t[2], place, b2, b1, b0, expand)


def kernel(x_prompt, x_sample, c_prompt, c_sample, state_ssm_re, state_ssm_im, cache_kv_w128, cache_kv_w512, cache_kv_w2048, w_ada, b_ada, ln_g, ln_b, ssm_w_in, ssm_a_re, ssm_a_im, ssm_log_dt, ssm_b_re, ssm_b_im, ssm_c_re, ssm_c_im, ssm_d, ssm_w_glu, ssm_b_glu, ssm_w_out, attn_w_in, attn_w_out):
    bsz, seq, d = x_prompt.shape
    dbsz, dec_seq, _ = x_sample.shape
    n_dec = dbsz * dec_seq
    att_width = ATT_HEADS * HEAD_DIM

    pad = (-(bsz + dbsz)) % 8
    c_all = jnp.concatenate([c_prompt, c_sample, jnp.zeros((pad, d), c_prompt.dtype)], 0)
    mod = _ada(c_all, w_ada, b_ada)
    bias_p = _prompt_bias()
    sample_tables = _sample_tables()
    caches_t = [jnp.transpose(c, (0, 1, 3, 4, 5, 2))
                for c in (cache_kv_w128, cache_kv_w512, cache_kv_w2048)]

    xp = x_prompt
    xs = x_sample.reshape(1, n_dec, d)
    ssm_p_re, ssm_p_im, ssm_s_re, ssm_s_im = [], [], [], []
    kv_p = [[], [], []]
    kv_s = [[], [], []]
    for i in range(DEPTH):
        j = i // 2
        mod_p = mod[i, :bsz]
        mod_s = jnp.repeat(mod[i, bsz:bsz + dbsz], dec_seq, axis=0)
        sh_p, sc_p, gt_p = [mod_p[:, None, k * d:(k + 1) * d] for k in range(3)]
        sh_s, sc_s, gt_s = [mod_s[None, :, k * d:(k + 1) * d] for k in range(3)]
        if i % 2 == 0:
            prep = _ssm_prepare(ssm_a_re[j], ssm_a_im[j], ssm_log_dt[j], ssm_b_re[j], ssm_b_im[j],
                                ssm_c_re[j], ssm_c_im[j], ssm_d[j], dec_seq)
            w_in = ssm_w_in[j].astype(BF16)
            w_s, w_z = w_in[:, :d], w_in[:, d:]
            w_glu = ssm_w_glu[j].astype(BF16)
            w_out = ssm_w_out[j].astype(BF16)

            s_p = _modmm(xp, sc_p, sh_p, w_s, 1024, d)
            y_p, h_p = _ssm_core_prompt(s_p, prep, 1024)
            hr, hi = _state_from_kt(h_p)
            ssm_p_re.append(hr)
            ssm_p_im.append(hi)
            xp = _tail(xp, sc_p, sh_p, gt_p, y_p, w_z, w_out, ln_g[i], ln_b[i], 512,
                       wglu_bf=w_glu, bglu=ssm_b_glu[j])

            s_s = _modmm(xs, sc_s, sh_s, w_s, n_dec, d)
            h0 = _state_to_kt(state_ssm_re[j], state_ssm_im[j])
            y_s, h_s = _ssm_core_sample(s_s[0], h0, prep, dec_seq)
            hr, hi = _state_from_kt(h_s.transpose(1, 0, 2))
            ssm_s_re.append(hr)
            ssm_s_im.append(hi)
            xs = _tail(xs, sc_s, sh_s, gt_s, y_s[None], w_z, w_out, ln_g[i], ln_b[i], 512,
                       wglu_bf=w_glu, bglu=ssm_b_glu[j])
        else:
            w_in = attn_w_in[j].astype(BF16)
            w_qkv, w_z = w_in[:, :ATT_QKV], w_in[:, ATT_QKV:]
            w_out = attn_w_out[j].astype(BF16)

            qkv, kv0, kv1, kv2 = _attn_inproj_prompt(xp, sc_p, sh_p, w_qkv)
            for g, kv in enumerate((kv0, kv1, kv2)):
                kv_p[g].append(kv.reshape(bsz, WINDOWS[g], 2, ATT_HEADS, HEAD_DIM))
            o_p = _attn_core_prompt(qkv, bias_p)
            xp = _tail(xp, sc_p, sh_p, gt_p, o_p, w_z, w_out, ln_g[i], ln_b[i], 512)

            proj_s = _modmm(xs, sc_s, sh_s, w_qkv, n_dec, 1024)[0].reshape(dbsz, dec_seq, ATT_QKV)
            for g in range(3):
                lo = (3 * g + 1) * att_width
                kv_s[g].append(proj_s[:, :, lo:lo + 2 * att_width]
                               .reshape(dbsz, dec_seq, 2, ATT_HEADS, HEAD_DIM))
            o_s = _attn_core_sample(proj_s, caches_t, sample_tables, j)
            xs = _tail(xs, sc_s, sh_s, gt_s, o_s.reshape(1, n_dec, att_width), w_z, w_out,
                       ln_g[i], ln_b[i], 512)

    return (xp, xs.reshape(dbsz, dec_seq, d),
            jnp.stack(ssm_p_re, 0), jnp.stack(ssm_p_im, 0),
            jnp.stack(kv_p[0], 0), jnp.stack(kv_p[1], 0), jnp.stack(kv_p[2], 0),
            jnp.stack(ssm_s_re, 0), jnp.stack(ssm_s_im, 0),
            jnp.stack(kv_s[0], 0), jnp.stack(kv_s[1], 0), jnp.stack(kv_s[2], 0))
```

```python
import functools

import numpy as np
import jax
import jax.numpy as jnp
from jax import lax
from jax.experimental import pallas as pl
from jax.experimental.pallas import tpu as pltpu

F32 = jnp.float32
BF16 = jnp.bfloat16

D_MODEL = 1024
DEPTH = 4
SSM_GROUPS = 64
SSM_GROUP = 16
SSM_STATE = 64
ATT_HEADS = 16
HEAD_DIM = 64
WINDOWS = (128, 512, 2048)
DILATIONS = (1, 4, 16)
ATT_QKV = 3 * 3 * ATT_HEADS * HEAD_DIM
ALIBI_BASE_EXP = 8.0
DEEPNORM_ALPHA = (2 * DEPTH) ** 0.25
LN_EPS = 1e-5
QK_SCALE = HEAD_DIM ** -0.5

LANE_TILE = 128
BLK = 128
SSM_CHUNK = 128
SSM_CENTER = SSM_CHUNK // 2
SSM_KT = D_MODEL // LANE_TILE
GROUPS_PER_KT = LANE_TILE // SSM_GROUP
KT_STATE = GROUPS_PER_KT * SSM_STATE
VMEM_LIMIT = 56 * 1024 * 1024


def _cparams(sem):
    return pltpu.CompilerParams(dimension_semantics=sem, vmem_limit_bytes=VMEM_LIMIT)


def _ada_body(c_ref, w_ref, b_ref, o_ref):
    c = c_ref[...]
    s = c * jax.nn.sigmoid(c)
    o_ref[0] = jnp.dot(s.astype(BF16), w_ref[0].astype(BF16), preferred_element_type=F32) + b_ref[0]


def _ada(c_all, w_ada, b_ada):
    rows = c_all.shape[0]
    d = D_MODEL
    return pl.pallas_call(
        _ada_body,
        grid=(DEPTH, 3),
        in_specs=[pl.BlockSpec((rows, d), lambda i, n: (0, 0)),
                  pl.BlockSpec((1, d, d), lambda i, n: (i, 0, n)),
                  pl.BlockSpec((1, 1, d), lambda i, n: (i, 0, n))],
        out_specs=pl.BlockSpec((1, rows, d), lambda i, n: (i, 0, n)),
        out_shape=jax.ShapeDtypeStruct((DEPTH, rows, 3 * d), F32),
        compiler_params=_cparams(("arbitrary", "arbitrary")),
        name="ada",
    )(c_all, w_ada, b_ada.reshape(DEPTH, 1, 3 * d))


def _modmm_body(x_ref, sc_ref, sh_ref, w_ref, o_ref, u_scr):
    @pl.when(pl.program_id(2) == 0)
    def _():
        u_scr[...] = (x_ref[0] * (1.0 + sc_ref[0]) + sh_ref[0]).astype(BF16)

    o_ref[0] = jnp.dot(u_scr[...], w_ref[...], preferred_element_type=F32)


def _mod_index(per_token):
    if per_token:
        return lambda b, m, n: (b, m, 0)
    return lambda b, m, n: (b, 0, 0)


def _modmm(x, sc, sh, w_bf, tm, tn):
    bsz, length, d = x.shape
    n_out = w_bf.shape[1]
    per_token = sc.shape[1] != 1
    mod_rows = tm if per_token else 1
    return pl.pallas_call(
        _modmm_body,
        grid=(bsz, length // tm, n_out // tn),
        in_specs=[pl.BlockSpec((1, tm, d), lambda b, m, n: (b, m, 0)),
                  pl.BlockSpec((1, mod_rows, d), _mod_index(per_token)),
                  pl.BlockSpec((1, mod_rows, d), _mod_index(per_token)),
                  pl.BlockSpec((d, tn), lambda b, m, n: (0, n))],
        out_specs=pl.BlockSpec((1, tm, tn), lambda b, m, n: (b, m, n)),
        out_shape=jax.ShapeDtypeStruct((bsz, length, n_out), F32),
        scratch_shapes=[pltpu.VMEM((tm, d), BF16)],
        compiler_params=_cparams(("arbitrary", "arbitrary", "arbitrary")),
        name="mod_proj",
    )(x, sc, sh, w_bf)


def _tail_body(*refs, glu):
    if glu:
        (x_ref, sc_ref, sh_ref, gt_ref, y_ref, wz_ref, wglu_ref, bglu_ref, wout_ref,
         lng_ref, lnb_ref, o_ref) = refs
    else:
        (x_ref, sc_ref, sh_ref, gt_ref, y_ref, wz_ref, wout_ref, lng_ref, lnb_ref, o_ref) = refs
    x = x_ref[0]
    u = (x * (1.0 + sc_ref[0]) + sh_ref[0]).astype(BF16)
    z = jnp.dot(u, wz_ref[...], preferred_element_type=F32)
    y = y_ref[0]
    if glu:
        y = y * jax.nn.sigmoid(
            jnp.dot(y.astype(BF16), wglu_ref[...], preferred_element_type=F32) + bglu_ref[...])
    y = y * (z * jax.nn.sigmoid(z))
    out = jnp.dot(y.astype(BF16), wout_ref[...], preferred_element_type=F32)
    r = DEEPNORM_ALPHA * x + gt_ref[0] * out
    mu = jnp.mean(r, axis=-1, keepdims=True)
    rc = r - mu
    var = jnp.mean(rc * rc, axis=-1, keepdims=True)
    o_ref[0] = rc * lax.rsqrt(var + LN_EPS) * lng_ref[...] + lnb_ref[...]


def _tail(x, sc, sh, gt, y, wz_bf, wout_bf, ln_g, ln_b, tm, wglu_bf=None, bglu=None):
    bsz, length, d = x.shape
    per_token = sc.shape[1] != 1
    mod_rows = tm if per_token else 1
    if per_token:
        mod_idx = lambda b, m: (b, m, 0)
    else:
        mod_idx = lambda b, m: (b, 0, 0)
    row_spec = pl.BlockSpec((1, tm, d), lambda b, m: (b, m, 0))
    mod_spec = pl.BlockSpec((1, mod_rows, d), mod_idx)
    w_spec = pl.BlockSpec((d, d), lambda b, m: (0, 0))
    vec_spec = pl.BlockSpec((1, d), lambda b, m: (0, 0))
    glu = wglu_bf is not None
    operands = [x, sc, sh, gt, y, wz_bf]
    in_specs = [row_spec, mod_spec, mod_spec, mod_spec, row_spec, w_spec]
    if glu:
        operands += [wglu_bf, bglu.reshape(1, d)]
        in_specs += [w_spec, vec_spec]
    operands += [wout_bf, ln_g.reshape(1, d), ln_b.reshape(1, d)]
    in_specs += [w_spec, vec_spec, vec_spec]
    return pl.pallas_call(
        functools.partial(_tail_body, glu=glu),
        grid=(bsz, length // tm),
        in_specs=in_specs,
        out_specs=row_spec,
        out_shape=jax.ShapeDtypeStruct((bsz, length, d), F32),
        compiler_params=_cparams(("arbitrary", "arbitrary")),
        name="tail_glu" if glu else "tail",
    )(*operands)


def _complex_mul(ar, ai, br, bi):
    return ar * br - ai * bi, ar * bi + ai * br


def _prefix_matmul(l_mat, g):
    hi = g.astype(BF16)
    lo = (g - hi.astype(F32)).astype(BF16)
    return (jnp.dot(l_mat, hi, preferred_element_type=F32)
            + jnp.dot(l_mat, lo, preferred_element_type=F32))


def _ssm_scaled_inputs(s_c, bdb, tn):
    ks = KT_STATE
    bu = jnp.dot(s_c.astype(BF16), bdb, preferred_element_type=F32)
    return _complex_mul(tn[:, :ks], tn[:, ks:], bu[:, :ks], bu[:, ks:])


def _ssm_outputs(s_c, p_re, p_im, tp, bdc, dsk):
    ks = KT_STATE
    h_re, h_im = _complex_mul(tp[:, :ks], tp[:, ks:], p_re, p_im)
    y = (jnp.dot(h_re.astype(BF16), bdc[:ks], preferred_element_type=F32)
         + jnp.dot(h_im.astype(BF16), bdc[ks:], preferred_element_type=F32))
    return jax.nn.gelu(y + dsk * s_c), h_re, h_im


def _ssm_prompt_body(s_ref, bdb_ref, bdc_ref, tn_ref, tp_ref, l_ref, cm_ref, d_ref,
                     y_ref, h_ref, carry_scr, *, n_chunks):
    ks = KT_STATE
    t = SSM_CHUNK

    @pl.when(pl.program_id(2) == 0)
    def _():
        carry_scr[...] = jnp.zeros_like(carry_scr)

    l_mat = l_ref[...]
    tn = tn_ref[0]
    tp = tp_ref[0]
    cm = cm_ref[0]
    bdb = bdb_ref[0]
    bdc = bdc_ref[0]
    dsk = d_ref[...]
    sums = []
    for ci in range(n_chunks):
        s_c = s_ref[0, ci * t:(ci + 1) * t, :]
        g_re, g_im = _ssm_scaled_inputs(s_c, bdb, tn)
        sums.append((_prefix_matmul(l_mat, g_re), _prefix_matmul(l_mat, g_im)))
    hp = carry_scr[0:1, :]
    carried = []
    for p_re, p_im in sums:
        c_re, c_im = _complex_mul(cm[:, :ks], cm[:, ks:], hp[:, :ks], hp[:, ks:])
        carried.append((c_re, c_im))
        e_re, e_im = _complex_mul(tp[t - 1:t, :ks], tp[t - 1:t, ks:],
                                  p_re[t - 1:t] + c_re, p_im[t - 1:t] + c_im)
        hp = jnp.concatenate([e_re, e_im], axis=1)
    for ci in range(n_chunks):
        s_c = s_ref[0, ci * t:(ci + 1) * t, :]
        (p_re, p_im), (c_re, c_im) = sums[ci], carried[ci]
        y, _, _ = _ssm_outputs(s_c, p_re + c_re, p_im + c_im, tp, bdc, dsk)
        y_ref[0, ci * t:(ci + 1) * t, :] = y
    carry_scr[0:1, :] = hp
    h_ref[0, 0] = jnp.broadcast_to(hp, (8, 2 * ks))


def _ssm_core_prompt(s, prep, tt):
    bsz, length, d = s.shape
    t = SSM_CHUNK
    w2 = 2 * KT_STATE
    y, h = pl.pallas_call(
        functools.partial(_ssm_prompt_body, n_chunks=tt // t),
        grid=(bsz, SSM_KT, length // tt),
        in_specs=[pl.BlockSpec((1, tt, LANE_TILE), lambda b, k, c: (b, c, k)),
                  pl.BlockSpec((1, LANE_TILE, w2), lambda b, k, c: (k, 0, 0)),
                  pl.BlockSpec((1, w2, LANE_TILE), lambda b, k, c: (k, 0, 0)),
                  pl.BlockSpec((1, t, w2), lambda b, k, c: (k, 0, 0)),
                  pl.BlockSpec((1, t, w2), lambda b, k, c: (k, 0, 0)),
                  pl.BlockSpec((t, t), lambda b, k, c: (0, 0)),
                  pl.BlockSpec((1, 1, w2), lambda b, k, c: (k, 0, 0)),
                  pl.BlockSpec((1, LANE_TILE), lambda b, k, c: (0, k))],
        out_specs=[pl.BlockSpec((1, tt, LANE_TILE), lambda b, k, c: (b, c, k)),
                   pl.BlockSpec((1, 1, 8, w2), lambda b, k, c: (b, k, 0, 0))],
        out_shape=[jax.ShapeDtypeStruct((bsz, length, d), F32),
                   jax.ShapeDtypeStruct((bsz, SSM_KT, 8, w2), F32)],
        scratch_shapes=[pltpu.VMEM((8, w2), F32)],
        compiler_params=_cparams(("arbitrary", "arbitrary", "arbitrary")),
        name="ssm_core_prompt",
    )(s, prep["bdb"], prep["bdc"], prep["tn_p"], prep["tp_p"], prep["l_p"], prep["cm_p"],
      prep["dsk"])
    return y, h[:, :, 0, :]


def _lane_tiles_store(scr, v):
    for c in range(v.shape[1] // LANE_TILE):
        scr[c] = v[:, c * LANE_TILE:(c + 1) * LANE_TILE]


def _lane_tiles_load(scr, rows):
    return jnp.concatenate([scr[c, rows, :] for c in range(scr.shape[0])], axis=1)


def _ssm_sample_body(s_ref, h0_ref, bdb_ref, bdc_ref, tn_ref, tp_ref, l_ref, cm_ref, d_ref,
                     y_ref, h_ref, gre_scr, gim_scr, *, dec_seq):
    ks = KT_STATE
    nb = SSM_CHUNK // dec_seq
    cm = cm_ref[0]
    h0 = h0_ref[0]
    s_c = s_ref[...]
    g_re, g_im = _ssm_scaled_inputs(s_c, bdb_ref[0], tn_ref[0])
    c_re, c_im = _complex_mul(cm[:, :ks], cm[:, ks:], h0[:, :ks], h0[:, ks:])
    first = pl.ds(0, nb, stride=dec_seq)
    _lane_tiles_store(gre_scr, g_re)
    _lane_tiles_store(gim_scr, g_im)
    for c in range(ks // LANE_TILE):
        cols = slice(c * LANE_TILE, (c + 1) * LANE_TILE)
        gre_scr[c, first, :] = gre_scr[c, first, :] + c_re[:, cols]
        gim_scr[c, first, :] = gim_scr[c, first, :] + c_im[:, cols]
    l_mat = l_ref[...]
    p_re = _prefix_matmul(l_mat, _lane_tiles_load(gre_scr, slice(None)))
    p_im = _prefix_matmul(l_mat, _lane_tiles_load(gim_scr, slice(None)))
    y, h_re, h_im = _ssm_outputs(s_c, p_re, p_im, tp_ref[0], bdc_ref[0], d_ref[...])
    y_ref[...] = y
    _lane_tiles_store(gre_scr, h_re)
    _lane_tiles_store(gim_scr, h_im)
    last = pl.ds(dec_seq - 1, nb, stride=dec_seq)
    h_ref[0, :, :ks] = _lane_tiles_load(gre_scr, last)
    h_ref[0, :, ks:] = _lane_tiles_load(gim_scr, last)


def _ssm_core_sample(s, h0, prep, dec_seq):
    rows, d = s.shape
    t = SSM_CHUNK
    nb = t // dec_seq
    w2 = 2 * KT_STATE
    return pl.pallas_call(
        functools.partial(_ssm_sample_body, dec_seq=dec_seq),
        grid=(SSM_KT, rows // t),
        in_specs=[pl.BlockSpec((t, LANE_TILE), lambda k, c: (c, k)),
                  pl.BlockSpec((1, nb, w2), lambda k, c: (k, c, 0)),
                  pl.BlockSpec((1, LANE_TILE, w2), lambda k, c: (k, 0, 0)),
                  pl.BlockSpec((1, w2, LANE_TILE), lambda k, c: (k, 0, 0)),
                  pl.BlockSpec((1, t, w2), lambda k, c: (k, 0, 0)),
                  pl.BlockSpec((1, t, w2), lambda k, c: (k, 0, 0)),
                  pl.BlockSpec((t, t), lambda k, c: (0, 0)),
                  pl.BlockSpec((1, 1, w2), lambda k, c: (k, 0, 0)),
                  pl.BlockSpec((1, LANE_TILE), lambda k, c: (0, k))],
        out_specs=[pl.BlockSpec((t, LANE_TILE), lambda k, c: (c, k)),
                   pl.BlockSpec((1, nb, w2), lambda k, c: (k, c, 0))],
        out_shape=[jax.ShapeDtypeStruct((rows, d), F32),
                   jax.ShapeDtypeStruct((SSM_KT, rows // dec_seq, w2), F32)],
        scratch_shapes=[pltpu.VMEM((KT_STATE // LANE_TILE, t, LANE_TILE), F32)] * 2,
        compiler_params=_cparams(("arbitrary", "arbitrary")),
        name="ssm_core_sample",
    )(s, h0, prep["bdb"], prep["bdc"], prep["tn_s"], prep["tp_s"], prep["l_s"], prep["cm_s"],
      prep["dsk"])


def _to_kt_lanes(re, im):
    rows = re.shape[0]

    def tile(v):
        return v.reshape(rows, SSM_KT, KT_STATE).transpose(1, 0, 2)

    return jnp.concatenate([tile(re), tile(im)], axis=-1)


def _ssm_prepare(a_re, a_im, log_dt, b_re, b_im, c_re, c_im, d_skip, dec_seq):
    dt = jnp.exp(log_dt)[:, None]
    lr, li = a_re, a_im
    mag = jnp.exp(lr * dt)
    abar_re = mag * jnp.cos(li * dt)
    abar_im = mag * jnp.sin(li * dt)
    den = lr * lr + li * li
    xr = abar_re - 1.0
    g_re = (xr * lr + abar_im * li) / den
    g_im = (abar_im * lr - xr * li) / den
    bbar_re = g_re[..., None] * b_re - g_im[..., None] * b_im
    bbar_im = g_re[..., None] * b_im + g_im[..., None] * b_re

    eye = jnp.eye(GROUPS_PER_KT, dtype=a_re.dtype)

    def in_map(bb):
        bb = bb.reshape(SSM_KT, GROUPS_PER_KT, SSM_STATE, SSM_GROUP)
        m = jnp.einsum("ab,kbpc->kacbp", eye, bb)
        return m.reshape(SSM_KT, LANE_TILE, KT_STATE)

    def out_map(cc):
        cc = cc.reshape(SSM_KT, GROUPS_PER_KT, SSM_GROUP, SSM_STATE)
        m = jnp.einsum("ab,kbcp->kapbc", eye, cc)
        return m.reshape(SSM_KT, KT_STATE, LANE_TILE)

    bdb = jnp.concatenate([in_map(bbar_re), in_map(bbar_im)], axis=-1).astype(BF16)
    bdc = jnp.concatenate([out_map(c_re), out_map(-c_im)], axis=1).astype(BF16)

    t = SSM_CHUNK
    c = SSM_CENTER
    log_mag = (lr * dt).reshape(SSM_KT, 1, KT_STATE)
    phase = (li * dt).reshape(SSM_KT, 1, KT_STATE)

    def power_rows(exps):
        k = jnp.asarray(exps, dtype=a_re.dtype).reshape(1, len(exps), 1)
        mag_k = jnp.exp(k * log_mag)
        return jnp.concatenate([mag_k * jnp.cos(k * phase), mag_k * jnp.sin(k * phase)], axis=-1)

    tn_p = power_rows([c - j for j in range(t)])
    tp_p = power_rows([i - c for i in range(t)])
    cm_p = power_rows([c + 1])
    tn_s = power_rows([-(r % dec_seq) for r in range(t)])
    tp_s = power_rows([r % dec_seq for r in range(t)])
    cm_s = power_rows([1])

    idx = jnp.arange(t)
    l_p = (idx[:, None] >= idx[None, :]).astype(BF16)
    l_s = ((idx[:, None] >= idx[None, :])
           & (idx[:, None] // dec_seq == idx[None, :] // dec_seq)).astype(BF16)
    return dict(bdb=bdb, bdc=bdc, tn_p=tn_p, tp_p=tp_p, cm_p=cm_p, tn_s=tn_s, tp_s=tp_s,
                cm_s=cm_s, l_p=l_p, l_s=l_s, dsk=d_skip.reshape(1, D_MODEL))


def _state_to_kt(h_re, h_im):
    return _to_kt_lanes(h_re, h_im)


def _state_from_kt(h):
    bsz = h.shape[0]
    re = h[:, :, :KT_STATE].reshape(bsz, SSM_GROUPS, SSM_STATE)
    im = h[:, :, KT_STATE:].reshape(bsz, SSM_GROUPS, SSM_STATE)
    return re, im


ATT_MT = 2048
ATT_TN = 512
QKV_TILES = ATT_QKV // ATT_TN
TILES_PER_GROUP = QKV_TILES // 3
KV_FIRST_TILE = TILES_PER_GROUP // 3
KV_TILES = 2 * TILES_PER_GROUP // 3
UNIT_UNROLL = 8


def _attn_inproj_body(x_ref, sc_ref, sh_ref, w_ref, qkv_ref, kv0_ref, kv1_ref, kv2_ref,
                      u_scr, acc_scr, *, last_mt):
    mt = pl.program_id(1)
    nt = pl.program_id(2)

    @pl.when(nt == 0)
    def _():
        u_scr[...] = (x_ref[0] * (1.0 + sc_ref[0]) + sh_ref[0]).astype(BF16)

    _lane_tiles_store(acc_scr, jnp.dot(u_scr[...], w_ref[...], preferred_element_type=F32))
    n_ct = ATT_TN // LANE_TILE
    group = nt // TILES_PER_GROUP
    kv_refs = (kv0_ref, kv1_ref, kv2_ref)
    for g in range(3):
        win, dil = WINDOWS[g], DILATIONS[g]

        @pl.when(group == g)
        def _(win=win, dil=dil):
            for c in range(n_ct):
                cols = slice(c * LANE_TILE, (c + 1) * LANE_TILE)
                if dil == 1:
                    qkv_ref[0, :, cols] = acc_scr[c].astype(BF16)
                    continue
                for n in range(ATT_MT // win):
                    for r in range(dil):
                        src = acc_scr[c, pl.ds(n * win + r, BLK, stride=dil), :]
                        dst = n * win + r * BLK
                        qkv_ref[0, dst:dst + BLK, cols] = src.astype(BF16)

        first = g * TILES_PER_GROUP + KV_FIRST_TILE

        @pl.when((mt == last_mt) & (nt >= first) & (nt < first + KV_TILES))
        def _(win=win, kv_ref=kv_refs[g]):
            for c in range(n_ct):
                kv_ref[0, :, c * LANE_TILE:(c + 1) * LANE_TILE] = acc_scr[c, ATT_MT - win:, :]


def _attn_inproj_prompt(x, sc, sh, w_bf):
    bsz, length, d = x.shape
    n_mt = length // ATT_MT
    last_mt = n_mt - 1
    kv_width = KV_TILES * ATT_TN

    def kv_index(g):
        first = g * TILES_PER_GROUP + KV_FIRST_TILE

        def index(b, m, n):
            col = jnp.clip(n - first, 0, KV_TILES - 1)
            return (b, 0, jnp.where(m == last_mt, col, 0))

        return index

    return pl.pallas_call(
        functools.partial(_attn_inproj_body, last_mt=last_mt),
        grid=(bsz, n_mt, QKV_TILES),
        in_specs=[pl.BlockSpec((1, ATT_MT, d), lambda b, m, n: (b, m, 0)),
                  pl.BlockSpec((1, 1, d), lambda b, m, n: (b, 0, 0)),
                  pl.BlockSpec((1, 1, d), lambda b, m, n: (b, 0, 0)),
                  pl.BlockSpec((d, ATT_TN), lambda b, m, n: (0, n))],
        out_specs=[pl.BlockSpec((1, ATT_MT, ATT_TN), lambda b, m, n: (b, m, n))]
        + [pl.BlockSpec((1, WINDOWS[g], ATT_TN), kv_index(g)) for g in range(3)],
        out_shape=[jax.ShapeDtypeStruct((bsz, length, ATT_QKV), BF16)]
        + [jax.ShapeDtypeStruct((bsz, WINDOWS[g], kv_width), F32) for g in range(3)],
        scratch_shapes=[pltpu.VMEM((ATT_MT, d), BF16), pltpu.VMEM((ATT_TN // LANE_TILE, ATT_MT, LANE_TILE), F32)],
        compiler_params=_cparams(("arbitrary", "arbitrary", "arbitrary")),
        name="attn_inproj_prompt",
    )(x, sc, sh, w_bf)


def _attn_core_body(q0, k0, v0, q1, k1, v1, q2, k2, v2, bias_ref, o_ref, o_scr, l_scr,
                    *, length):
    qkv = ((q0, k0, v0), (q1, k1, v1), (q2, k2, v2))
    lane = lax.broadcasted_iota(jnp.int32, (1, LANE_TILE), 1)
    head_lanes = (lane < HEAD_DIM, lane >= HEAD_DIM)
    nt_dims = (((1,), (1,)), ((), ()))
    for g in range(3):
        win, dil = WINDOWS[g], DILATIONS[g]
        q_ref, k_ref, v_ref = qkv[g]
        for r in range(dil):
            def unit(n, carry, g=g, win=win, dil=dil, r=r, q_ref=q_ref, k_ref=k_ref, v_ref=v_ref):
                cur = pl.multiple_of(n * win + r * BLK, BLK)
                prev = pl.multiple_of(jnp.maximum(n - 1, 0) * win + r * BLK, BLK)
                q = q_ref[0, pl.ds(cur, BLK), :] * QK_SCALE
                k = jnp.concatenate([k_ref[0, pl.ds(prev, BLK), :], k_ref[0, pl.ds(cur, BLK), :]], 0)
                v = jnp.concatenate([v_ref[0, pl.ds(prev, BLK), :], v_ref[0, pl.ds(cur, BLK), :]], 0)
                variant = jnp.minimum(n, 1)
                outs, lses = [], []
                for hh in range(2):
                    qm = jnp.where(head_lanes[hh], q, jnp.zeros_like(q))
                    s = lax.dot_general(qm, k, nt_dims, preferred_element_type=F32)
                    s = s + bias_ref[g, variant, hh]
                    m = jnp.max(jnp.maximum(s[:, :BLK], s[:, BLK:]), axis=-1, keepdims=True)
                    p = jnp.exp(s - m)
                    l = jnp.sum(p[:, :BLK] + p[:, BLK:], axis=-1, keepdims=True)
                    outs.append(jnp.dot(p.astype(BF16), v, preferred_element_type=F32) / l)
                    lses.append(m + jnp.log(l))
                o = jnp.where(head_lanes[0], outs[0], outs[1])
                lse = jnp.where(head_lanes[0], lses[0], lses[1])
                rows = pl.ds(pl.multiple_of(n * win, BLK) + r, BLK, stride=dil)
                o_scr[g, rows, :] = o
                l_scr[g, rows, :] = lse
                return carry

            lax.fori_loop(0, length // win, unit, 0, unroll=min(length // win, UNIT_UNROLL))

    merge_rows = 512
    for c in range(length // merge_rows):
        rows = slice(c * merge_rows, (c + 1) * merge_rows)
        l0, l1, l2 = l_scr[0, rows, :], l_scr[1, rows, :], l_scr[2, rows, :]
        m = jnp.maximum(jnp.maximum(l0, l1), l2)
        e0, e1, e2 = jnp.exp(l0 - m), jnp.exp(l1 - m), jnp.exp(l2 - m)
        num = e0 * o_scr[0, rows, :] + e1 * o_scr[1, rows, :] + e2 * o_scr[2, rows, :]
        o_ref[0, rows, :] = num / (e0 + e1 + e2)


def _alibi_slopes():
    h = np.arange(ATT_HEADS, dtype=np.float32) + np.float32(1.0)
    return np.exp2(-np.float32(ALIBI_BASE_EXP) * h / np.float32(ATT_HEADS)).astype(np.float32)


def _prompt_bias():
    slopes = _alibi_slopes()
    a = np.arange(BLK)[:, None]
    bk = np.arange(2 * BLK)[None, :]
    steps = a - bk + BLK
    ok = (steps >= 0) & (steps <= BLK)
    ok_first = ok & (bk >= BLK)
    neg_inf = np.float32(-np.inf)
    tables = []
    for dil in DILATIONS:
        bias = -slopes[:, None, None] * (steps * dil).astype(np.float32)[None]
        tables.append(np.stack([np.where(ok_first[None], bias, neg_inf),
                                np.where(ok[None], bias, neg_inf)], 0))
    return jnp.asarray(np.stack(tables, 0))


def _attn_core_prompt(qkv, bias):
    bsz, length, _ = qkv.shape
    n_hp = ATT_HEADS // 2
    col_tiles = ATT_HEADS * HEAD_DIM // LANE_TILE

    def col_spec(g, part):
        base = (g * 3 + part) * col_tiles
        return pl.BlockSpec((1, length, LANE_TILE), lambda b, hp: (b, 0, base + hp))

    in_specs = [col_spec(g, part) for g in range(3) for part in range(3)]
    in_specs.append(pl.BlockSpec((3, 2, 2, BLK, 2 * BLK), lambda b, hp: (0, 0, hp, 0, 0)))
    return pl.pallas_call(
        functools.partial(_attn_core_body, length=length),
        grid=(bsz, n_hp),
        in_specs=in_specs,
        out_specs=pl.BlockSpec((1, length, LANE_TILE), lambda b, hp: (b, 0, hp)),
        out_shape=jax.ShapeDtypeStruct((bsz, length, ATT_HEADS * HEAD_DIM), F32),
        scratch_shapes=[pltpu.VMEM((3, length, LANE_TILE), F32),
                        pltpu.VMEM((3, length, LANE_TILE), F32)],
        compiler_params=_cparams(("arbitrary", "arbitrary")),
        name="attn_core_prompt",
    )(*([qkv] * 9), bias)


SA_HEADS = 8
SA_ROWS = SA_HEADS * HEAD_DIM
DEC_SEQ = 8
SA_SLOTS = 4
NEG_BIG = -1e30

SA_PASSES = [("g2", 2, 16), ("g1a", 1, 4), ("g1b", 1, 4)]
SA_Q_TILE = {name: i for i, (name, _, _) in enumerate(SA_PASSES)}
SA_NEW = ("g2", "g1a", "g1b", "g0")
SA_K_TILE = {name: len(SA_PASSES) + i for i, name in enumerate(SA_NEW)}
SA_V_TILE = {name: len(SA_PASSES) + len(SA_NEW) + i for i, name in enumerate(SA_NEW)}
SA_TILES = len(SA_PASSES) + 2 * len(SA_NEW)


def _src_lane(part, g, t):
    return (part * SA_SLOTS + g) * DEC_SEQ + t


def _sample_tables():
    lam = np.arange(LANE_TILE)
    none = np.full(LANE_TILE, -1)

    def tile(src):
        m = np.zeros((LANE_TILE, LANE_TILE), np.float32)
        ok = src >= 0
        m[src[ok], lam[ok]] = 1.0
        return m

    def new_src(part, name):
        if name == "g2":
            return np.where(lam < 8, _src_lane(part, 2, lam % 8), none)
        if name == "g1a":
            return np.where(lam < 4, _src_lane(part, 1, lam % 4), none)
        if name == "g1b":
            return np.where(lam < 4, _src_lane(part, 1, 4 + lam % 4),
                            np.where(lam < 8, _src_lane(part, 1, lam % 4), none))
        return np.where(lam < 8, _src_lane(part, 0, lam % 8), none)

    tiles = []
    for name, g, _ in SA_PASSES:
        if name == "g2":
            tiles.append(tile(np.where(lam % 16 < 8, _src_lane(0, 2, lam % 8), none)))
        elif name == "g1a":
            tiles.append(tile(_src_lane(0, 1, lam % 4)))
        else:
            tiles.append(tile(_src_lane(0, 1, 4 + lam % 4)))
    for part in (1, 2):
        for name in SA_NEW:
            tiles.append(tile(new_src(part, name)))
    place = np.concatenate(tiles, axis=1)

    slopes = _alibi_slopes()[:, None]

    def bias(dist, ok):
        d = np.where(ok, dist, 0).astype(np.float32)[None, :]
        return np.where(ok[None, :], -slopes * d, np.float32(-np.inf)).astype(np.float32)

    def table(win, cache_dist, cache_ok, new_dist, new_ok):
        rho = np.arange(win)
        return np.concatenate([bias(cache_dist(rho), cache_ok(rho)),
                               bias(new_dist(lam), new_ok(lam))], axis=1)

    zero = lambda l: 0 * l
    b2 = table(2048, lambda r: 2048 - 16 * (r // 16), lambda r: r % 16 < 8, zero, lambda l: l < 8)
    b1a = table(512, lambda r: 512 - 4 * (r // 4), lambda r: r >= 0, zero, lambda l: l < 4)
    b1b = table(512, lambda r: 516 - 4 * (r // 4), lambda r: r >= 4,
                lambda l: np.where(l < 4, 0, 4), lambda l: l < 8)
    b0 = [table(128, lambda r, t=t: 128 + t - r, lambda r, t=t: r >= t,
                lambda l, t=t: t - l, lambda l, t=t: l <= t) for t in range(DEC_SEQ)]
    expand = np.zeros((LANE_TILE, SA_ROWS), np.float32)
    for hh in range(SA_HEADS):
        expand[hh, hh * HEAD_DIM:(hh + 1) * HEAD_DIM] = 1.0
    return (jnp.asarray(place, BF16), jnp.asarray(b2), jnp.asarray(np.stack([b1a, b1b], 0)),
            jnp.asarray(np.stack(b0, 1)), jnp.asarray(expand, BF16))


def _class_reduce(x, op, period):
    if period == 1:
        red = jnp.max if op is jnp.maximum else jnp.sum
        return jnp.broadcast_to(red(x, axis=1, keepdims=True), x.shape)
    shift = period
    while shift < LANE_TILE:
        x = op(x, pltpu.roll(x, shift, axis=1))
        shift *= 2
    return x


def _split_dot(a, b_bf):
    hi = a.astype(BF16)
    lo = (a - hi.astype(F32)).astype(BF16)
    return (jnp.dot(hi, b_bf, preferred_element_type=F32)
            + jnp.dot(lo, b_bf, preferred_element_type=F32))


def _attn_sample_body(xt_ref, q0_ref, c0_ref, c1_ref, c2_ref, place_ref, b2_ref, b1_ref, b0_ref,
                      exp_ref, o_ref, pl_scr, s_scr, r_scr, st_scr):
    lt = LANE_TILE
    xt = xt_ref[0].astype(BF16)
    for c in range(SA_TILES):
        pl_scr[c] = jnp.dot(xt, place_ref[:, c * lt:(c + 1) * lt], preferred_element_type=F32)
    st_scr[...] = jnp.zeros_like(st_scr)
    lane = lax.broadcasted_iota(jnp.int32, (1, lt), 1)
    caches = (c0_ref, c1_ref, c2_ref)
    biases = {"g2": lambda: b2_ref[...], "g1a": lambda: b1_ref[0], "g1b": lambda: b1_ref[1]}
    nt_dims = (((1,), (1,)), ((), ()))

    def head_rows(h):
        return slice(h * HEAD_DIM, (h + 1) * HEAD_DIM)

    base = 0
    for slot, (name, g, period) in enumerate(SA_PASSES):
        win = WINDOWS[g]
        n_ct = win // lt
        cache = caches[g]
        q_tile, k_tile, v_tile = SA_Q_TILE[name], SA_K_TILE[name], SA_V_TILE[name]

        def cols(j, base=base):
            return slice(base + j * lt, base + (j + 1) * lt)

        for h in range(SA_HEADS):
            q = pl_scr[q_tile, head_rows(h), :]
            for j in range(n_ct):
                kt = cache[0, 0, 0, h, :, j * lt:(j + 1) * lt]
                s_scr[h:h + 1, cols(j)] = jnp.sum(q * kt, axis=0, keepdims=True)
            kn = pl_scr[k_tile, head_rows(h), :]
            s_scr[h:h + 1, cols(n_ct)] = jnp.sum(q * kn, axis=0, keepdims=True)

        bias = biases[name]()
        tiles = [s_scr[:, cols(j)] * QK_SCALE + bias[:, j * lt:(j + 1) * lt]
                 for j in range(n_ct + 1)]
        m = _class_reduce(functools.reduce(jnp.maximum, tiles), jnp.maximum, period)
        m = jnp.maximum(m, NEG_BIG)
        ps = [jnp.exp(tl - m) for tl in tiles]
        l = _class_reduce(functools.reduce(lambda a, b: a + b, ps), lambda a, b: a + b, period)
        l = jnp.maximum(l, -1.0 / NEG_BIG)
        inv = 1.0 / l
        for j in range(n_ct + 1):
            s_scr[:, cols(j)] = ps[j] * inv
        st_scr[slot, 0:SA_HEADS, :] = m + jnp.log(l)

        for h in range(SA_HEADS):
            acc = s_scr[h:h + 1, cols(n_ct)] * pl_scr[v_tile, head_rows(h), :]
            for j in range(n_ct):
                acc = acc + s_scr[h:h + 1, cols(j)] * cache[0, 0, 1, h, :, j * lt:(j + 1) * lt]
            r_scr[slot, head_rows(h), :] = acc
        base += win + lt

    q0 = q0_ref[0].astype(BF16)
    l0 = jnp.zeros((DEC_SEQ, lt), F32)
    o0_heads = []
    for h in range(SA_HEADS):
        kt = jnp.concatenate([c0_ref[0, 0, 0, h], pl_scr[SA_K_TILE["g0"], head_rows(h), :]], axis=1)
        vt = jnp.concatenate([c0_ref[0, 0, 1, h], pl_scr[SA_V_TILE["g0"], head_rows(h), :]], axis=1)
        s = jnp.dot(q0[:, head_rows(h)], kt.astype(BF16), preferred_element_type=F32)
        s = s * QK_SCALE + b0_ref[h]
        m = jnp.max(s, axis=1, keepdims=True)
        p = jnp.exp(s - m)
        l = jnp.sum(p, axis=1, keepdims=True)
        o0_heads.append(lax.dot_general(p.astype(BF16), vt.astype(BF16), nt_dims,
                                        preferred_element_type=F32) / l)
        l0 = jnp.where(lane == h, m + jnp.log(l), l0)
    o0 = jnp.concatenate(o0_heads, axis=1)

    def by_query(slot):
        rt = r_scr[slot].T
        return functools.reduce(lambda a, b: a + b,
                                [rt[8 * j:8 * (j + 1), :] for j in range(lt // 8)])

    def stats_by_query(slot):
        return st_scr[slot].T[0:DEC_SEQ, :]

    row = lax.broadcasted_iota(jnp.int32, (DEC_SEQ, 1), 0)
    low = row < 4
    o2 = by_query(0)
    oa, ob = by_query(1), by_query(2)
    o1 = jnp.where(low, oa + pltpu.roll(oa, 4, axis=0), ob + pltpu.roll(ob, 4, axis=0))
    l2 = stats_by_query(0)
    l1 = jnp.where(low, stats_by_query(1), pltpu.roll(stats_by_query(2), 4, axis=0))
    mx = jnp.maximum(jnp.maximum(l0, l1), l2)
    e0, e1, e2 = jnp.exp(l0 - mx), jnp.exp(l1 - mx), jnp.exp(l2 - mx)
    den = e0 + e1 + e2
    w = jnp.concatenate([e0 / den, e1 / den, e2 / den], axis=0)
    w = _split_dot(w, exp_ref[...])
    o_ref[0] = (w[0:DEC_SEQ] * o0 + w[DEC_SEQ:2 * DEC_SEQ] * o1 + w[2 * DEC_SEQ:] * o2)


def _attn_core_sample(proj, caches_t, tables, layer):
    dbsz, tlen, _ = proj.shape
    width = ATT_HEADS * HEAD_DIM
    n_hb = ATT_HEADS // SA_HEADS
    xt = jnp.pad(proj.reshape(dbsz, tlen, 3, 3, width),
                 ((0, 0), (0, 0), (0, SA_SLOTS - 3), (0, SA_SLOTS - 3), (0, 0)))
    xt = xt.transpose(0, 4, 3, 2, 1).reshape(dbsz, width, SA_SLOTS * SA_SLOTS * tlen)
    place, b2, b1, b0, expand = tables

    def cache_spec(win):
        return pl.BlockSpec((1, 1, 2, SA_HEADS, HEAD_DIM, win),
                            lambda b, hb: (layer, b, 0, hb, 0, 0))

    score_lanes = sum(WINDOWS[g] + LANE_TILE for _, g, _ in SA_PASSES)
    n_pass = len(SA_PASSES)
    return pl.pallas_call(
        _attn_sample_body,
        grid=(dbsz, n_hb),
        in_specs=[pl.BlockSpec((1, SA_ROWS, LANE_TILE), lambda b, hb: (b, hb, 0)),
                  pl.BlockSpec((1, tlen, SA_ROWS), lambda b, hb: (b, 0, hb)),
                  cache_spec(WINDOWS[0]), cache_spec(WINDOWS[1]), cache_spec(WINDOWS[2]),
                  pl.BlockSpec(place.shape, lambda b, hb: (0, 0)),
                  pl.BlockSpec((SA_HEADS, b2.shape[1]), lambda b, hb: (hb, 0)),
                  pl.BlockSpec((2, SA_HEADS, b1.shape[2]), lambda b, hb: (0, hb, 0)),
                  pl.BlockSpec((SA_HEADS, DEC_SEQ, b0.shape[2]), lambda b, hb: (hb, 0, 0)),
                  pl.BlockSpec(expand.shape, lambda b, hb: (0, 0))],
        out_specs=pl.BlockSpec((1, tlen, SA_ROWS), lambda b, hb: (b, 0, hb)),
        out_shape=jax.ShapeDtypeStruct((dbsz, tlen, width), F32),
        scratch_shapes=[pltpu.VMEM((SA_TILES, SA_ROWS, LANE_TILE), F32),
                        pltpu.VMEM((SA_HEADS, score_lanes), F32),
                        pltpu.VMEM((n_pass, SA_ROWS, LANE_TILE), F32),
                        pltpu.VMEM((n_pass, LANE_TILE, LANE_TILE), F32)],
        compiler_params=_cparams(("arbitrary", "arbitrary")),
        name="attn_core_sample",
    )(xt, proj, caches_t[0], caches_t[1], caches_t[2], place, b2, b1, b0, expand)


def kernel(x_prompt, x_sample, c_prompt, c_sample, state_ssm_re, state_ssm_im, cache_kv_w128, cache_kv_w512, cache_kv_w2048, w_ada, b_ada, ln_g, ln_b, ssm_w_in, ssm_a_re, ssm_a_im, ssm_log_dt, ssm_b_re, ssm_b_im, ssm_c_re, ssm_c_im, ssm_d, ssm_w_glu, ssm_b_glu, ssm_w_out, attn_w_in, attn_w_out):
    bsz, seq, d = x_prompt.shape
    dbsz, dec_seq, _ = x_sample.shape
    n_dec = dbsz * dec_seq
    att_width = ATT_HEADS * HEAD_DIM

    pad = (-(bsz + dbsz)) % 8
    c_all = jnp.concatenate([c_prompt, c_sample, jnp.zeros((pad, d), c_prompt.dtype)], 0)
    mod = _ada(c_all, w_ada, b_ada)
    bias_p = _prompt_bias()
    sample_tables = _sample_tables()
    caches_t = [jnp.transpose(c, (0, 1, 3, 4, 5, 2))
                for c in (cache_kv_w128, cache_kv_w512, cache_kv_w2048)]

    xp = x_prompt
    xs = x_sample.reshape(1, n_dec, d)
    ssm_p_re, ssm_p_im, ssm_s_re, ssm_s_im = [], [], [], []
    kv_p = [[], [], []]
    kv_s = [[], [], []]
    for i in range(DEPTH):
        j = i // 2
        mod_p = mod[i, :bsz]
        mod_s = jnp.repeat(mod[i, bsz:bsz + dbsz], dec_seq, axis=0)
        sh_p, sc_p, gt_p = [mod_p[:, None, k * d:(k + 1) * d] for k in range(3)]
        sh_s, sc_s, gt_s = [mod_s[None, :, k * d:(k + 1) * d] for k in range(3)]
        if i % 2 == 0:
            prep = _ssm_prepare(ssm_a_re[j], ssm_a_im[j], ssm_log_dt[j], ssm_b_re[j], ssm_b_im[j],
                                ssm_c_re[j], ssm_c_im[j], ssm_d[j], dec_seq)
            w_in = ssm_w_in[j].astype(BF16)
            w_s, w_z = w_in[:, :d], w_in[:, d:]
            w_glu = ssm_w_glu[j].astype(BF16)
            w_out = ssm_w_out[j].astype(BF16)

            s_p = _modmm(xp, sc_p, sh_p, w_s, 1024, d)
            y_p, h_p = _ssm_core_prompt(s_p, prep, 1024)
            hr, hi = _state_from_kt(h_p)
            ssm_p_re.append(hr)
            ssm_p_im.append(hi)
            xp = _tail(xp, sc_p, sh_p, gt_p, y_p, w_z, w_out, ln_g[i], ln_b[i], 512,
                       wglu_bf=w_glu, bglu=ssm_b_glu[j])

            s_s = _modmm(xs, sc_s, sh_s, w_s, n_dec, d)
            h0 = _state_to_kt(state_ssm_re[j], state_ssm_im[j])
            y_s, h_s = _ssm_core_sample(s_s[0], h0, prep, dec_seq)
            hr, hi = _state_from_kt(h_s.transpose(1, 0, 2))
            ssm_s_re.append(hr)
            ssm_s_im.append(hi)
            xs = _tail(xs, sc_s, sh_s, gt_s, y_s[None], w_z, w_out, ln_g[i], ln_b[i], 512,
                       wglu_bf=w_glu, bglu=ssm_b_glu[j])
        else:
            w_in = attn_w_in[j].astype(BF16)
            w_qkv, w_z = w_in[:, :ATT_QKV], w_in[:, ATT_QKV:]
            w_out = attn_w_out[j].astype(BF16)

            qkv, kv0, kv1, kv2 = _attn_inproj_prompt(xp, sc_p, sh_p, w_qkv)
            for g, kv in enumerate((kv0, kv1, kv2)):
                kv_p[g].append(kv.reshape(bsz, WINDOWS[g], 2, ATT_HEADS, HEAD_DIM))
            o_p = _attn_core_prompt(qkv, bias_p)
            xp = _tail(xp, sc_p, sh_p, gt_p, o_p, w_z, w_out, ln_g[i], ln_b[i], 512)

            proj_s = _modmm(xs, sc_s, sh_s, w_qkv, n_dec, 1024)[0].reshape(dbsz, dec_seq, ATT_QKV)
            for g in range(3):
                lo = (3 * g + 1) * att_width
                kv_s[g].append(proj_s[:, :, lo:lo + 2 * att_width]
                               .reshape(dbsz, dec_seq, 2, ATT_HEADS, HEAD_DIM))
            o_s = _attn_core_sample(proj_s, caches_t, sample_tables, j)
            xs = _tail(xs, sc_s, sh_s, gt_s, o_s.reshape(1, n_dec, att_width), w_z, w_out,
                       ln_g[i], ln_b[i], 512)

    return (xp, xs.reshape(dbsz, dec_seq, d),
            jnp.stack(ssm_p_re, 0), jnp.stack(ssm_p_im, 0),
            jnp.stack(kv_p[0], 0), jnp.stack(kv_p[1], 0), jnp.stack(kv_p[2], 0),
            jnp.stack(ssm_s_re, 0), jnp.stack(ssm_s_im, 0),
            jnp.stack(kv_s[0], 0), jnp.stack(kv_s[1], 0), jnp.stack(kv_s[2], 0))
```
